```python
import jax, jax.numpy as jnp
from jax import lax
import numpy as np

D_MODEL = 4096
BATCH = 4
SEQ = 2048
DEPTH = 2
DEC_BATCH = 8
DEC_SEQ = 1
PAST_LEN = 16384
PAGE_SIZE = 128

N_MIXERS = 2
N_SB_LAYERS = (DEPTH + 1) // 2
N_RWKV_LAYERS = DEPTH // 2
SB_HEADS = 32
SB_HEAD_DIM = D_MODEL // SB_HEADS
SB_SCALE = SB_HEAD_DIM ** -0.5
SB_BIAS_INIT = -6.0
Q_BLOCK = 128
RWKV_HEAD_DIM = 64
RWKV_HEADS = D_MODEL // RWKV_HEAD_DIM
DECAY_LORA = max(32, int(round(1.8 * D_MODEL ** 0.5 / 32)) * 32)
AAA_LORA = max(32, int(round(1.8 * D_MODEL ** 0.5 / 32)) * 32)
GATE_LORA = max(32, int(round(0.6 * D_MODEL ** 0.8 / 32)) * 32)
D_FF = 4 * D_MODEL
RMS_EPS = 1e-5
GN_EPS = 64e-5

kernel_name = "stickbreak_rwkv7_hybrid_step"


def rmsnorm(x, g):
    xf = x.astype(jnp.float32)
    y = xf * lax.rsqrt(jnp.mean(jnp.square(xf), axis=-1, keepdims=True) + RMS_EPS)
    return (y * g.astype(jnp.float32)).astype(x.dtype)


def sb_attention(q, q_pos, segments, bias):
    B, Tq, H, Dh = q.shape
    blk = Q_BLOCK if Tq % Q_BLOCK == 0 else Tq
    nb = Tq // blk
    qb = jnp.swapaxes(q.reshape(B, nb, blk, H, Dh), 0, 1)
    pb = q_pos.reshape(nb, blk)
    k_pos = jnp.concatenate([p for (_, _, p) in segments])
    sizes = [kk.shape[1] for (kk, _, _) in segments]
    b_h = bias.astype(jnp.float32)[None, :, None, None]

    def one_block(args):
        qq, pp = args
        z = jnp.concatenate([jnp.einsum('bqhd,bkhd->bhqk', qq, kk, preferred_element_type=jnp.float32)
                             for (kk, _, _) in segments], axis=3) * SB_SCALE + b_h
        mask = k_pos[None, :] < pp[:, None]
        log_rest = jnp.where(mask, jax.nn.log_sigmoid(-z), 0.0)
        between = lax.cumsum(log_rest, axis=3, reverse=True) - log_rest
        att = jnp.where(mask, jnp.exp(jax.nn.log_sigmoid(z) + between), 0.0)
        out = 0.0
        off = 0
        for (kk, vv, _), n in zip(segments, sizes):
            out = out + jnp.einsum('bhqk,bkhd->bqhd', att[..., off:off + n].astype(vv.dtype), vv,
                                   preferred_element_type=jnp.float32)
            off += n
        return out.astype(q.dtype)

    o = lax.map(one_block, (qb, pb))
    return jnp.swapaxes(o, 0, 1).reshape(B, Tq, H, Dh)


def sb_mixer(h, q_pos, past, w_qkv, w_o, bias):
    B, T, D = h.shape
    qkv = (h @ w_qkv).reshape(B, T, 3, SB_HEADS, SB_HEAD_DIM)
    q, k, v = qkv[:, :, 0], qkv[:, :, 1], qkv[:, :, 2]
    segments = [(k, v, q_pos)] if past is None else [past, (k, v, q_pos)]
    o = sb_attention(q, q_pos, segments, bias)
    return o.reshape(B, T, D) @ w_o, k, v


def _wkv_step(S, inp):
    r_t, w_t, k_t, v_t, a_t, b_t = inp
    sa = jnp.einsum('bhij,bhj->bhi', S, a_t)
    S = S * w_t[:, :, None, :] + sa[..., None] * b_t[:, :, None, :] + v_t[..., None] * k_t[:, :, None, :]
    return S, jnp.einsum('bhij,bhj->bhi', S, r_t)


def rwkv_time_mix(h, shift_prev, wkv0, mu, w_r, w_k, w_v, w_o, w0, w1, w2, a0, a1, a2, g1, g2,
                  k_k, k_a, r_k, ln_w, ln_b):
    B, T, D = h.shape
    f32 = jnp.float32
    x_prev = jnp.concatenate([shift_prev[:, None, :].astype(h.dtype), h[:, :-1]], axis=1)
    xx = x_prev - h
    xr, xw, xk, xv, xa, xg = [h + xx * mu[i] for i in range(6)]
    r = (xr @ w_r).astype(f32)
    k = (xk @ w_k).astype(f32)
    v = (xv @ w_v).astype(f32)
    w = -jax.nn.softplus(-(w0 + jnp.tanh(xw @ w1) @ w2).astype(f32)) - 0.5
    a = jax.nn.sigmoid((a0 + (xa @ a1) @ a2).astype(f32))
    g = jax.nn.sigmoid(xg @ g1) @ g2
    heads = lambda t: t.reshape(B, T, RWKV_HEADS, RWKV_HEAD_DIM)
    kk = heads(k * k_k)
    kk = kk / jnp.maximum(jnp.sqrt(jnp.sum(jnp.square(kk), axis=-1, keepdims=True)), 1e-12)
    k = heads(k * (1.0 + (a - 1.0) * k_a))
    r, v, a, decay = heads(r), heads(v), heads(a), heads(jnp.exp(-jnp.exp(w)))
    seq_first = lambda t: jnp.swapaxes(t, 0, 1)
    xs = tuple(seq_first(t) for t in (r, decay, k, v, -kk, kk * a))
    wkv_T, y = lax.scan(_wkv_step, wkv0.astype(f32), xs)
    y = seq_first(y)
    mean = jnp.mean(y, axis=-1, keepdims=True)
    var = jnp.mean(jnp.square(y - mean), axis=-1, keepdims=True)
    y = ((y - mean) * lax.rsqrt(var + GN_EPS)).reshape(B, T, D) * ln_w + ln_b
    bonus = (jnp.sum(r * k * r_k, axis=-1, keepdims=True) * v).reshape(B, T, D)
    out = ((y + bonus) * g.astype(f32)).astype(h.dtype) @ w_o
    return out, h[:, -1, :], wkv_T


def sq_relu_mlp(h, w1, w2):
    return jnp.square(jax.nn.relu(h @ w1)) @ w2


def setup_inputs(seed: int = 0) -> dict:
    key = jax.random.key(seed)
    ks = iter(jax.random.split(key, 40))

    def nrm(shape, scale=1.0):
        return jax.random.normal(next(ks), shape, jnp.float32) * scale

    n_pages = PAST_LEN // PAGE_SIZE
    n_used = DEC_BATCH * n_pages
    n_pool = n_used + (n_used + 3) // 4
    D, L, R = D_MODEL, N_SB_LAYERS, N_RWKV_LAYERS
    inp = {}
    inp['x_prompt'] = nrm((BATCH, SEQ, D))
    inp['x_sample'] = nrm((DEC_BATCH, DEC_SEQ, D))
    inp['cache_sb_k'] = nrm((L, n_pool, PAGE_SIZE, SB_HEADS, SB_HEAD_DIM))
    inp['cache_sb_v'] = nrm((L, n_pool, PAGE_SIZE, SB_HEADS, SB_HEAD_DIM))
    inp['page_table'] = jax.random.permutation(next(ks), n_pool)[:n_used].reshape(DEC_BATCH, n_pages).astype(jnp.int32)
    inp['state_rwkv_wkv'] = nrm((R, DEC_BATCH, RWKV_HEADS, RWKV_HEAD_DIM, RWKV_HEAD_DIM), 0.3)
    inp['state_rwkv_shift'] = nrm((R, DEC_BATCH, D))
    inp['norm_mix'] = 1.0 + nrm((DEPTH, D), 0.02)
    inp['norm_mlp'] = 1.0 + nrm((DEPTH, D), 0.02)
    inp['norm_final'] = 1.0 + nrm((D,), 0.02)
    inp['sb_w_qkv'] = nrm((L, D, 3 * D), D ** -0.5)
    inp['sb_w_o'] = nrm((L, D, D), D ** -0.5)
    inp['sb_bias'] = SB_BIAS_INIT + nrm((L, SB_HEADS), 0.1)
    inp['rw_mu'] = jax.random.uniform(next(ks), (R, 6, D), jnp.float32)
    inp['rw_w_r'] = nrm((R, D, D), D ** -0.5)
    inp['rw_w_k'] = nrm((R, D, D), D ** -0.5)
    inp['rw_w_v'] = nrm((R, D, D), D ** -0.5)
    inp['rw_w_o'] = nrm((R, D, D), D ** -0.5)
    inp['rw_w0'] = jax.random.uniform(next(ks), (R, D), jnp.float32, -3.0, 1.0)
    inp['rw_w1'] = nrm((R, D, DECAY_LORA), D ** -0.5)
    inp['rw_w2'] = nrm((R, DECAY_LORA, D), 0.5 * DECAY_LORA ** -0.5)
    inp['rw_a0'] = nrm((R, D), 0.1)
    inp['rw_a1'] = nrm((R, D, AAA_LORA), D ** -0.5)
    inp['rw_a2'] = nrm((R, AAA_LORA, D), 0.5 * AAA_LORA ** -0.5)
    inp['rw_g1'] = nrm((R, D, GATE_LORA), D ** -0.5)
    inp['rw_g2'] = nrm((R, GATE_LORA, D), GATE_LORA ** -0.5)
    inp['rw_k_k'] = 0.85 + nrm((R, D), 0.05)
    inp['rw_k_a'] = 1.0 + nrm((R, D), 0.05)
    inp['rw_r_k'] = nrm((R, RWKV_HEADS, RWKV_HEAD_DIM), 0.1)
    inp['rw_ln_w'] = 1.0 + nrm((R, D), 0.02)
    inp['rw_ln_b'] = nrm((R, D), 0.02)
    inp['mlp_w1'] = nrm((DEPTH, D, D_FF), D ** -0.5)
    inp['mlp_w2'] = nrm((DEPTH, D_FF, D), D_FF ** -0.5)
    return inp


def reference(x_prompt, x_sample, cache_sb_k, cache_sb_v, page_table, state_rwkv_wkv, state_rwkv_shift,
              norm_mix, norm_mlp, norm_final, sb_w_qkv, sb_w_o, sb_bias,
              rw_mu, rw_w_r, rw_w_k, rw_w_v, rw_w_o, rw_w0, rw_w1, rw_w2, rw_a0, rw_a1, rw_a2,
              rw_g1, rw_g2, rw_k_k, rw_k_a, rw_r_k, rw_ln_w, rw_ln_b, mlp_w1, mlp_w2):
    n_pages = page_table.shape[1]
    past_len = n_pages * PAGE_SIZE
    T_p = x_prompt.shape[1]
    T_s = x_sample.shape[1]
    B_p, B_s = x_prompt.shape[0], x_sample.shape[0]
    pos_p = jnp.arange(T_p, dtype=jnp.int32)
    pos_s = past_len + jnp.arange(T_s, dtype=jnp.int32)
    pos_past = jnp.arange(past_len, dtype=jnp.int32)

    yp, ys = x_prompt, x_sample
    kp_l, vp_l, ks_l, vs_l = [], [], [], []
    wkvp_l, shp_l, wkvs_l, shs_l = [], [], [], []
    for layer in range(DEPTH):
        li = layer // N_MIXERS
        hp = rmsnorm(yp, norm_mix[layer])
        hs = rmsnorm(ys, norm_mix[layer])
        if layer % N_MIXERS == 0:
            op, kp, vp = sb_mixer(hp, pos_p, None, sb_w_qkv[li], sb_w_o[li], sb_bias[li])
            k_past = cache_sb_k[li][page_table].reshape(B_s, past_len, SB_HEADS, SB_HEAD_DIM)
            v_past = cache_sb_v[li][page_table].reshape(B_s, past_len, SB_HEADS, SB_HEAD_DIM)
            os_, ks, vs = sb_mixer(hs, pos_s, (k_past.astype(hs.dtype), v_past.astype(hs.dtype), pos_past),
                                   sb_w_qkv[li], sb_w_o[li], sb_bias[li])
            kp_l.append(kp); vp_l.append(vp); ks_l.append(ks); vs_l.append(vs)
        else:
            prm = (rw_mu[li], rw_w_r[li], rw_w_k[li], rw_w_v[li], rw_w_o[li], rw_w0[li], rw_w1[li], rw_w2[li],
                   rw_a0[li], rw_a1[li], rw_a2[li], rw_g1[li], rw_g2[li], rw_k_k[li], rw_k_a[li], rw_r_k[li],
                   rw_ln_w[li], rw_ln_b[li])
            zero_shift = jnp.zeros((B_p, D_MODEL), hp.dtype)
            zero_wkv = jnp.zeros((B_p, RWKV_HEADS, RWKV_HEAD_DIM, RWKV_HEAD_DIM), jnp.float32)
            op, shp, wkvp = rwkv_time_mix(hp, zero_shift, zero_wkv, *prm)
            os_, shs, wkvs = rwkv_time_mix(hs, state_rwkv_shift[li], state_rwkv_wkv[li], *prm)
            wkvp_l.append(wkvp); shp_l.append(shp); wkvs_l.append(wkvs); shs_l.append(shs)
        yp = yp + op
        ys = ys + os_
        yp = yp + sq_relu_mlp(rmsnorm(yp, norm_mlp[layer]), mlp_w1[layer], mlp_w2[layer])
        ys = ys + sq_relu_mlp(rmsnorm(ys, norm_mlp[layer]), mlp_w1[layer], mlp_w2[layer])
    y_prompt = rmsnorm(yp, norm_final)
    y_sample = rmsnorm(ys, norm_final)
    new_k_prompt = jnp.stack(kp_l)
    new_v_prompt = jnp.stack(vp_l)
    new_k_sample = jnp.stack(ks_l)
    new_v_sample = jnp.stack(vs_l)
    new_wkv_prompt = jnp.stack(wkvp_l)
    new_shift_prompt = jnp.stack(shp_l)
    new_wkv_sample = jnp.stack(wkvs_l)
    new_shift_sample = jnp.stack(shs_l)
    return (y_prompt, y_sample, new_k_prompt, new_v_prompt, new_k_sample, new_v_sample,
            new_wkv_prompt, new_shift_prompt, new_wkv_sample, new_shift_sample)
```

```python
import functools

import jax
import jax.numpy as jnp
from jax import lax
from jax.experimental import pallas as pl
from jax.experimental.pallas import tpu as pltpu

F32 = jnp.float32
BF16 = jnp.bfloat16

LANES = 128
SUBLANES = 8
VMEM_LIMIT_BYTES = 56 * 1024 * 1024

PAGE_SIZE = 128
SB_HEAD_DIM = 128
RWKV_HEAD_DIM = 64
RMS_EPS = 1e-5
GN_EPS = 64e-5


def _params(*sem):
    return pltpu.CompilerParams(dimension_semantics=sem, vmem_limit_bytes=VMEM_LIMIT_BYTES)


def _rmsnorm_kernel(x_ref, g_ref, o_ref):
    x = x_ref[...]
    ms = jnp.mean(x * x, axis=-1, keepdims=True)
    o_ref[...] = ((x * lax.rsqrt(ms + RMS_EPS)) * g_ref[...]).astype(o_ref.dtype)


def rmsnorm(x, g, out_dtype, tm=256):
    m, d = x.shape
    tm = min(tm, m)
    assert m % tm == 0
    return pl.pallas_call(
        _rmsnorm_kernel,
        grid=(m // tm,),
        in_specs=[pl.BlockSpec((tm, d), lambda i: (i, 0)), pl.BlockSpec((1, d), lambda i: (0, 0))],
        out_specs=pl.BlockSpec((tm, d), lambda i: (i, 0)),
        out_shape=jax.ShapeDtypeStruct((m, d), out_dtype),
        compiler_params=_params("parallel"),
        name="rmsnorm",
    )(x, g.reshape(1, d).astype(F32))


def _mm_kernel(*refs, nk, act, has_res, n_out):
    x_ref, w_ref = refs[0], refs[1]
    res_ref = refs[2] if has_res else None
    first_out = 2 + int(has_res)
    out_refs = refs[first_out:first_out + n_out]
    acc_ref = refs[first_out + n_out] if nk > 1 else None

    def finish(acc):
        if act == "relu2":
            r = jnp.maximum(acc, 0.0)
            acc = r * r
        elif act == "tanh":
            acc = jnp.tanh(acc)
        elif act == "sigmoid":
            acc = 1.0 / (1.0 + jnp.exp(-acc))
        if has_res:
            acc = res_ref[...] + acc
        for o in out_refs:
            o[...] = acc.astype(o.dtype)

    part = jnp.dot(x_ref[...], w_ref[...], preferred_element_type=F32)
    if nk == 1:
        finish(part)
    else:
        k = pl.program_id(2)

        @pl.when(k == 0)
        def _():
            acc_ref[...] = part

        @pl.when(k > 0)
        def _():
            acc_ref[...] += part

        @pl.when(k == nk - 1)
        def _():
            finish(acc_ref[...])


def matmul(x, w, *, n_off=0, n=None, act=None, res=None, out_dtypes=(F32,), tm=1024, tn=512, tk=None):
    m, kdim = x.shape
    n = w.shape[1] if n is None else n
    tm, tn = min(tm, m), min(tn, n)
    tk = kdim if tk is None else min(tk, kdim)
    assert m % tm == 0 and n % tn == 0 and kdim % tk == 0 and n_off % tn == 0
    nk, joff = kdim // tk, n_off // tn
    in_specs = [pl.BlockSpec((tm, tk), lambda i, j, k: (i, k)),
                pl.BlockSpec((tk, tn), lambda i, j, k: (k, j + joff))]
    args = [x, w]
    if res is not None:
        in_specs.append(pl.BlockSpec((tm, tn), lambda i, j, k: (i, j)))
        args.append(res)
    out = pl.pallas_call(
        functools.partial(_mm_kernel, nk=nk, act=act, has_res=res is not None, n_out=len(out_dtypes)),
        grid=(m // tm, n // tn, nk),
        in_specs=in_specs,
        out_specs=[pl.BlockSpec((tm, tn), lambda i, j, k: (i, j)) for _ in out_dtypes],
        out_shape=[jax.ShapeDtypeStruct((m, n), dt) for dt in out_dtypes],
        scratch_shapes=[pltpu.VMEM((tm, tn), F32)] if nk > 1 else [],
        compiler_params=_params("parallel", "parallel", "arbitrary"),
        name="matmul",
    )(*args)
    return out[0] if len(out_dtypes) == 1 else tuple(out)


def _log_sigmoid_pair(z):
    t = jnp.log1p(jnp.exp(-jnp.abs(z)))
    return jnp.minimum(z, 0.0) - t, jnp.minimum(-z, 0.0) - t


def _suffix_sums(log_rest, u):
    hi = log_rest.astype(BF16)
    lo = (log_rest - hi.astype(F32)).astype(BF16)
    return (jnp.dot(hi, u, preferred_element_type=F32) + jnp.dot(lo, u, preferred_element_type=F32))


def _sb_prompt_kernel(bias_ref, q_ref, k_ref, v_ref, u_ref, o_ref, *, bq, bk, scale):
    h = pl.program_id(1)
    qi = pl.program_id(2)
    bias = bias_ref[h]
    q = q_ref[...]
    u = u_ref[...]
    nsub = bq // bk

    def block(kj, carry, acc, masked):
        start = pl.multiple_of(kj * bk, bk)
        k = k_ref[pl.ds(start, bk), :]
        v = v_ref[pl.ds(start, bk), :]
        z = lax.dot_general(q, k, (((1,), (1,)), ((), ())), preferred_element_type=F32) * scale + bias
        ls_pos, ls_neg = _log_sigmoid_pair(z)
        if masked:
            row = qi * bq + lax.broadcasted_iota(jnp.int32, (bq, bk), 0)
            col = kj * bk + lax.broadcasted_iota(jnp.int32, (bq, bk), 1)
            mask = col < row
            ls_neg = jnp.where(mask, ls_neg, 0.0)
        cs = _suffix_sums(ls_neg, u)
        p = jnp.exp(ls_pos + (cs[:, :bk] + carry))
        if masked:
            p = jnp.where(mask, p, 0.0)
        acc = acc + jnp.dot(p.astype(BF16), v, preferred_element_type=F32)
        return carry + cs[:, bk:], acc

    carry = jnp.zeros((bq, bk), F32)
    acc = jnp.zeros((bq, SB_HEAD_DIM), F32)
    for s in range(nsub):
        carry, acc = block(qi * nsub + (nsub - 1 - s), carry, acc, True)

    def body(i, ca):
        return block(qi * nsub - 1 - i, ca[0], ca[1], False)

    carry, acc = lax.fori_loop(0, qi * nsub, body, (carry, acc))
    o_ref[...] = acc.astype(o_ref.dtype)


def _suffix_matrix():
    j = lax.broadcasted_iota(jnp.int32, (LANES, 2 * LANES), 0)
    s = lax.broadcasted_iota(jnp.int32, (LANES, 2 * LANES), 1)
    return jnp.where((s >= LANES) | (j > s), 1.0, 0.0).astype(BF16)


def sb_attention_prompt(q, k, v, bias, b, t):
    n_heads = q.shape[1] // SB_HEAD_DIM
    bk = LANES
    bq = 256 if t % 256 == 0 else bk
    assert t % bq == 0
    nq = t // bq
    grid_spec = pltpu.PrefetchScalarGridSpec(
        num_scalar_prefetch=1,
        grid=(b, n_heads, nq),
        in_specs=[pl.BlockSpec((bq, SB_HEAD_DIM), lambda bi, h, i, _: (bi * nq + i, h)),
                  pl.BlockSpec((t, SB_HEAD_DIM), lambda bi, h, i, _: (bi, h)),
                  pl.BlockSpec((t, SB_HEAD_DIM), lambda bi, h, i, _: (bi, h)),
                  pl.BlockSpec((LANES, 2 * LANES), lambda bi, h, i, _: (0, 0))],
        out_specs=pl.BlockSpec((bq, SB_HEAD_DIM), lambda bi, h, i, _: (bi * nq + i, h)),
    )
    return pl.pallas_call(
        functools.partial(_sb_prompt_kernel, bq=bq, bk=bk, scale=SB_HEAD_DIM ** -0.5),
        grid_spec=grid_spec,
        out_shape=jax.ShapeDtypeStruct(q.shape, BF16),
        compiler_params=_params("parallel", "parallel", "parallel"),
        name="sb_attention_prompt",
    )(bias.astype(F32), q, k, v, _suffix_matrix())


def _sb_sample_kernel(pt_ref, qbd_ref, bias_ref, u_ref, knew_ref, vnew_ref, kc_ref, vc_ref, o_ref,
                      carry_ref, acc_ref, *, n_pages, n_new, scale):
    p = pl.program_id(1)
    q_pos = n_pages * PAGE_SIZE
    qbd = qbd_ref[...]

    @pl.when(p == 0)
    def _():
        carry_ref[...] = jnp.zeros_like(carry_ref)
        acc_ref[...] = jnp.zeros_like(acc_ref)

    def process(k_page, v_page, pos0, n_valid):
        kb = k_page.astype(BF16)
        z = lax.dot_general(qbd, kb, (((1,), (1,)), ((), ())), preferred_element_type=F32)
        z = z * scale + bias_ref[...]
        ls_pos, ls_neg = _log_sigmoid_pair(z)
        idx = lax.broadcasted_iota(jnp.int32, z.shape, 1)
        mask = ((pos0 + idx) < q_pos) & (idx < n_valid)
        ls_neg = jnp.where(mask, ls_neg, 0.0)
        cs = _suffix_sums(ls_neg, u_ref[...])
        att = jnp.where(mask, jnp.exp(ls_pos + (cs[:, :PAGE_SIZE] + carry_ref[...])), 0.0)
        acc_ref[...] += jnp.dot(att.astype(BF16), v_page.astype(BF16), preferred_element_type=F32)
        carry_ref[...] += cs[:, PAGE_SIZE:]

    @pl.when(p == 0)
    def _():
        process(knew_ref[...], vnew_ref[...], q_pos, n_new)

    @pl.when(p > 0)
    def _():
        process(kc_ref[...], vc_ref[...], (n_pages - p) * PAGE_SIZE, PAGE_SIZE)

    @pl.when(p == n_pages)
    def _():
        acc = acc_ref[...]
        row = lax.broadcasted_iota(jnp.int32, acc.shape, 0)
        col = lax.broadcasted_iota(jnp.int32, acc.shape, 1)
        own = (col // SB_HEAD_DIM) == row
        o_ref[...] = jnp.sum(jnp.where(own, acc, 0.0), axis=0, keepdims=True)


def sb_attention_sample(q, k_new, v_new, cache_k, cache_v, page_table, bias):
    bs, d = q.shape
    n_heads = d // SB_HEAD_DIM
    n_pages = page_table.shape[1]
    qh = q.reshape(bs, n_heads, 1, SB_HEAD_DIM)
    eye = jnp.eye(n_heads, dtype=F32).reshape(1, n_heads, n_heads, 1)
    qbd = (qh * eye).reshape(bs, n_heads, d).astype(BF16)
    pad = lambda a: jnp.pad(a.reshape(bs, 1, d), ((0, 0), (0, PAGE_SIZE - 1), (0, 0)))
    bias_b = jnp.broadcast_to(bias.astype(F32).reshape(n_heads, 1), (n_heads, PAGE_SIZE))
    page_of = lambda b, p, pt: pt[b, n_pages - jnp.maximum(p, 1)]
    grid_spec = pltpu.PrefetchScalarGridSpec(
        num_scalar_prefetch=1,
        grid=(bs, n_pages + 1),
        in_specs=[pl.BlockSpec((None, n_heads, d), lambda b, p, pt: (b, 0, 0)),
                  pl.BlockSpec((n_heads, PAGE_SIZE), lambda b, p, pt: (0, 0)),
                  pl.BlockSpec((LANES, 2 * LANES), lambda b, p, pt: (0, 0)),
                  pl.BlockSpec((None, PAGE_SIZE, d), lambda b, p, pt: (b, 0, 0)),
                  pl.BlockSpec((None, PAGE_SIZE, d), lambda b, p, pt: (b, 0, 0)),
                  pl.BlockSpec((None, PAGE_SIZE, d), lambda b, p, pt: (page_of(b, p, pt), 0, 0)),
                  pl.BlockSpec((None, PAGE_SIZE, d), lambda b, p, pt: (page_of(b, p, pt), 0, 0))],
        out_specs=pl.BlockSpec((None, 1, d), lambda b, p, pt: (b, 0, 0)),
        scratch_shapes=[pltpu.VMEM((n_heads, PAGE_SIZE), F32), pltpu.VMEM((n_heads, d), F32)],
    )
    out = pl.pallas_call(
        functools.partial(_sb_sample_kernel, n_pages=n_pages, n_new=1, scale=SB_HEAD_DIM ** -0.5),
        grid_spec=grid_spec,
        out_shape=jax.ShapeDtypeStruct((bs, 1, d), F32),
        compiler_params=_params("parallel", "arbitrary"),
        name="sb_attention_sample",
    )(page_table, qbd, bias_b, _suffix_matrix(), pad(k_new), pad(v_new), cache_k, cache_v)
    return out.reshape(bs, d)


def _head_minor_cols(w, n_heads):
    kdim = w.shape[0]
    return w.reshape(kdim, n_heads, RWKV_HEAD_DIM).swapaxes(1, 2).reshape(kdim, n_heads * RWKV_HEAD_DIM)


def _head_minor_vec(p, n_heads):
    return p.reshape(n_heads, RWKV_HEAD_DIM).T.reshape(1, n_heads * RWKV_HEAD_DIM).astype(F32)


def _mix_kernel(h_ref, xp_ref, mu_ref, *o_refs):
    h = h_ref[...]
    xx = xp_ref[...] - h
    for i, o in enumerate(o_refs):
        o[...] = (h + xx * mu_ref[i:i + 1, :]).astype(o.dtype)


def token_shift_mix(h, x_prev, mu, tm=128):
    m, d = h.shape
    tm = min(tm, m)
    assert m % tm == 0
    n_mix = mu.shape[0]
    row = pl.BlockSpec((tm, d), lambda i: (i, 0))
    return pl.pallas_call(
        _mix_kernel,
        grid=(m // tm,),
        in_specs=[row, row, pl.BlockSpec((n_mix, d), lambda i: (0, 0))],
        out_specs=[row] * n_mix,
        out_shape=[jax.ShapeDtypeStruct((m, d), BF16)] * n_mix,
        compiler_params=_params("parallel"),
        name="token_shift_mix",
    )(h, x_prev, mu.astype(F32))


def _head_sum(x):
    s = x[:, 0:LANES]
    for m in range(1, x.shape[1] // LANES):
        s = s + x[:, m * LANES:(m + 1) * LANES]
    return s + pltpu.roll(s, LANES // 2, 1)


def _tile_lanes(s, d):
    return jnp.concatenate([s] * (d // LANES), axis=1)


def _rwkv_prep_kernel(r_ref, k_ref, wl_ref, al_ref, prm_ref, decay_ref, k2_ref, kneg_ref, bb_ref, rk_ref):
    d = r_ref.shape[1]
    w0, a0, k_k, k_a, r_k = (prm_ref[i:i + 1, :] for i in range(5))
    u = -(w0 + wl_ref[...])
    softplus = jnp.maximum(u, 0.0) + jnp.log1p(jnp.exp(-jnp.abs(u)))
    decay_ref[...] = jnp.exp(-jnp.exp(-softplus - 0.5))
    a = 1.0 / (1.0 + jnp.exp(-(a0 + al_ref[...])))
    k = k_ref[...]
    kk = k * k_k
    norm = jnp.maximum(jnp.sqrt(_tile_lanes(_head_sum(kk * kk), d)), 1e-12)
    kk = kk / norm
    k2 = k * (1.0 + (a - 1.0) * k_a)
    k2_ref[...] = k2
    kneg_ref[...] = -kk
    bb_ref[...] = kk * a
    rk_ref[...] = _head_sum(r_ref[...] * k2 * r_k)


def rwkv_prep(r, k, wl, al, prm, tm=64):
    m, d = r.shape
    tm = min(tm, m)
    assert m % tm == 0
    row = pl.BlockSpec((tm, d), lambda i: (i, 0))
    full = jax.ShapeDtypeStruct((m, d), F32)
    return pl.pallas_call(
        _rwkv_prep_kernel,
        grid=(m // tm,),
        in_specs=[row, row, row, row, pl.BlockSpec(prm.shape, lambda i: (0, 0))],
        out_specs=[row, row, row, row, pl.BlockSpec((tm, LANES), lambda i: (i, 0))],
        out_shape=[full, full, full, full, jax.ShapeDtypeStruct((m, LANES), F32)],
        compiler_params=_params("parallel"),
        name="rwkv_prep",
    )(r, k, wl, al, prm)


def _wkv_kernel(s0_ref, w_ref, k_ref, b_ref, a_ref, r_ref, v_ref, y_ref, st_ref, s_ref, *, tc, nq, nm):
    c = pl.program_id(1)

    @pl.when(c == 0)
    def _():
        s_ref[...] = s0_ref[...]

    lane = lax.broadcasted_iota(jnp.int32, (SUBLANES, LANES), 1)
    lo_half = lane < (LANES // 2)
    swap = lambda x: pltpu.roll(x, LANES // 2, 1)

    def row(ref, t, m):
        return jnp.broadcast_to(ref[t, m:m + 1, :], (SUBLANES, LANES))

    def step(t, _):
        rows = [(q, par) for q in range(nq) for par in range(2)]
        sa = {}
        for m in range(nm):
            a_m = row(a_ref, t, m)
            for qp in rows:
                pr = s_ref[qp[0], qp[1], m] * a_m
                key = (qp, m % 2)
                sa[key] = pr if key not in sa else sa[key] + pr
        vt = {}
        for q in range(nq):
            x = v_ref[t, q * SUBLANES:(q + 1) * SUBLANES, :]
            xs = swap(x)
            vt[(q, 0)] = jnp.where(lo_half, x, xs)
            vt[(q, 1)] = jnp.where(lo_half, xs, x)
        for qp in rows:
            tot = sa[(qp, 0)] + sa[(qp, 1)] if nm > 1 else sa[(qp, 0)]
            sa[qp] = tot + swap(tot)
        ya = {}
        for m in range(nm):
            w_m, b_m, k_m, r_m = row(w_ref, t, m), row(b_ref, t, m), row(k_ref, t, m), row(r_ref, t, m)
            for qp in rows:
                s_new = s_ref[qp[0], qp[1], m] * w_m + sa[qp] * b_m + vt[qp] * k_m
                s_ref[qp[0], qp[1], m] = s_new
                pr = s_new * r_m
                key = (qp, m % 2)
                ya[key] = pr if key not in ya else ya[key] + pr
        for q in range(nq):
            ys = []
            for par in range(2):
                tot = ya[((q, par), 0)] + ya[((q, par), 1)] if nm > 1 else ya[((q, par), 0)]
                ys.append(tot + swap(tot))
            y_ref[t, q * SUBLANES:(q + 1) * SUBLANES, :] = jnp.where(lo_half, ys[0], ys[1])
        return 0

    lax.fori_loop(0, tc, step, 0)

    @pl.when(c == pl.num_programs(1) - 1)
    def _():
        st_ref[...] = s_ref[...]


def wkv_scan(s0, w, k, b, a, r, v, nb, t):
    d = w.shape[1]
    nq, nm = s0.shape[1], s0.shape[3]
    tc = 32 if t % 32 == 0 else t
    assert t % tc == 0 and (tc % SUBLANES == 0 or tc == t)
    seq3 = lambda x: x.reshape(nb, t, d // LANES, LANES)
    col3 = pl.BlockSpec((None, tc, d // LANES, LANES), lambda bi, c: (bi, c, 0, 0))
    st = pl.BlockSpec((None,) + s0.shape[1:], lambda bi, c: (bi, 0, 0, 0, 0, 0))
    y, s_t = pl.pallas_call(
        functools.partial(_wkv_kernel, tc=tc, nq=nq, nm=nm),
        grid=(nb, t // tc),
        in_specs=[st, col3, col3, col3, col3, col3, col3],
        out_specs=[col3, st],
        out_shape=[jax.ShapeDtypeStruct((nb, t, d // LANES, LANES), F32), jax.ShapeDtypeStruct(s0.shape, F32)],
        scratch_shapes=[pltpu.VMEM(s0.shape[1:], F32)],
        compiler_params=_params("parallel", "arbitrary"),
        name="wkv_scan",
    )(s0, seq3(w), seq3(k), seq3(b), seq3(a), seq3(r), seq3(v))
    return y.reshape(nb * t, d), s_t


def _state_to_kernel(s):
    nb, nh, n, _ = s.shape
    x = s.reshape(nb, nh, n // 16, SUBLANES, 2, n // 2, 2)
    x = x.transpose(0, 2, 4, 5, 3, 6, 1)
    return x.reshape(nb, n // 16, 2, n // 2, SUBLANES, 2 * nh)


def _state_from_kernel(x, nh):
    nb, nq, _, nm, _, _ = x.shape
    x = x.reshape(nb, nq, 2, nm, SUBLANES, 2, nh)
    x = x.transpose(0, 6, 1, 4, 2, 3, 5)
    return x.reshape(nb, nh, nq * 16, nm * 2)


def _rwkv_post_kernel(y_ref, v_ref, g_ref, rk_ref, ln_ref, o_ref):
    d = y_ref.shape[1]
    n = RWKV_HEAD_DIM
    y = y_ref[...]
    mean = _tile_lanes(_head_sum(y), d) * (1.0 / n)
    yc = y - mean
    var = _tile_lanes(_head_sum(yc * yc), d) * (1.0 / n)
    yn = (yc * lax.rsqrt(var + GN_EPS)) * ln_ref[0:1, :] + ln_ref[1:2, :]
    bonus = _tile_lanes(rk_ref[...], d) * v_ref[...]
    o_ref[...] = ((yn + bonus) * g_ref[...]).astype(o_ref.dtype)


def rwkv_post(y, v, g, rk, ln, tm=128):
    m, d = y.shape
    tm = min(tm, m)
    assert m % tm == 0
    row = pl.BlockSpec((tm, d), lambda i: (i, 0))
    return pl.pallas_call(
        _rwkv_post_kernel,
        grid=(m // tm,),
        in_specs=[row, row, row, pl.BlockSpec((tm, LANES), lambda i: (i, 0)), pl.BlockSpec(ln.shape, lambda i: (0, 0))],
        out_specs=row,
        out_shape=jax.ShapeDtypeStruct((m, d), BF16),
        compiler_params=_params("parallel"),
        name="rwkv_post",
    )(y, v, g, rk, ln)


def rwkv_time_mix(h, shift_prev, s0, wts, nb, t):
    d = h.shape[1]
    nh = d // RWKV_HEAD_DIM
    h3 = h.reshape(nb, t, d)
    x_prev = jnp.concatenate([shift_prev[:, None, :].astype(F32), h3[:, :-1]], axis=1).reshape(nb * t, d)
    xr, xw, xk, xv, xa, xg = token_shift_mix(h, x_prev, wts["mu"])
    r = matmul(xr, wts["w_r"])
    k = matmul(xk, wts["w_k"])
    v = matmul(xv, wts["w_v"])
    wl = matmul(matmul(xw, wts["w1"], act="tanh", out_dtypes=(BF16,)), wts["w2"])
    al = matmul(matmul(xa, wts["a1"], out_dtypes=(BF16,)), wts["a2"])
    g = matmul(matmul(xg, wts["g1"], act="sigmoid", out_dtypes=(BF16,)), wts["g2"])
    decay, k2, kneg, bb, rk = rwkv_prep(r, k, wl, al, wts["prm"])
    y, s_t = wkv_scan(_state_to_kernel(s0.astype(F32)), decay, k2, bb, kneg, r, v, nb, t)
    out = rwkv_post(y, v, g, rk, wts["ln"])
    return out, _state_from_kernel(s_t, nh)


def kernel(x_prompt, x_sample, cache_sb_k, cache_sb_v, page_table, state_rwkv_wkv, state_rwkv_shift, norm_mix, norm_mlp, norm_final, sb_w_qkv, sb_w_o, sb_bias, rw_mu, rw_w_r, rw_w_k, rw_w_v, rw_w_o, rw_w0, rw_w1, rw_w2, rw_a0, rw_a1, rw_a2, rw_g1, rw_g2, rw_k_k, rw_k_a, rw_r_k, rw_ln_w, rw_ln_b, mlp_w1, mlp_w2):
    bp, tp, d = x_prompt.shape
    bs, ts, _ = x_sample.shape
    assert ts == 1, "the paged decode attention handles one new token per sequence"
    depth = norm_mix.shape[0]
    nh_rw = d // RWKV_HEAD_DIM
    n_sb_heads = d // SB_HEAD_DIM

    yp = x_prompt.reshape(bp * tp, d)
    ys = x_sample.reshape(bs * ts, d)
    kp_l, vp_l, ks_l, vs_l = [], [], [], []
    wkvp_l, shp_l, wkvs_l, shs_l = [], [], [], []

    for layer in range(depth):
        li = layer // 2
        if layer % 2 == 0:
            w_qkv = sb_w_qkv[li].astype(BF16)
            w_o = sb_w_o[li].astype(BF16)
            bias = sb_bias[li]
            hp = rmsnorm(yp, norm_mix[layer], BF16)
            hs = rmsnorm(ys, norm_mix[layer], BF16)
            qp = matmul(hp, w_qkv, n_off=0, n=d, out_dtypes=(BF16,))
            kp, kp16 = matmul(hp, w_qkv, n_off=d, n=d, out_dtypes=(F32, BF16))
            vp, vp16 = matmul(hp, w_qkv, n_off=2 * d, n=d, out_dtypes=(F32, BF16))
            op = sb_attention_prompt(qp, kp16, vp16, bias, bp, tp)
            yp = matmul(op, w_o, res=yp)
            qs = matmul(hs, w_qkv, n_off=0, n=d)
            ks = matmul(hs, w_qkv, n_off=d, n=d)
            vs = matmul(hs, w_qkv, n_off=2 * d, n=d)
            n_pool = cache_sb_k.shape[1]
            os_ = sb_attention_sample(qs, ks, vs, cache_sb_k[li].reshape(n_pool, PAGE_SIZE, d),
                                      cache_sb_v[li].reshape(n_pool, PAGE_SIZE, d), page_table, bias)
            ys = matmul(os_.astype(BF16), w_o, res=ys)
            kp_l.append(kp.reshape(bp, tp, n_sb_heads, SB_HEAD_DIM))
            vp_l.append(vp.reshape(bp, tp, n_sb_heads, SB_HEAD_DIM))
            ks_l.append(ks.reshape(bs, ts, n_sb_heads, SB_HEAD_DIM))
            vs_l.append(vs.reshape(bs, ts, n_sb_heads, SB_HEAD_DIM))
        else:
            hm = lambda w: _head_minor_cols(w, nh_rw).astype(BF16)
            wts = {
                "mu": rw_mu[li],
                "w_r": hm(rw_w_r[li]), "w_k": hm(rw_w_k[li]), "w_v": hm(rw_w_v[li]),
                "w1": rw_w1[li].astype(BF16), "w2": hm(rw_w2[li]),
                "a1": rw_a1[li].astype(BF16), "a2": hm(rw_a2[li]),
                "g1": rw_g1[li].astype(BF16), "g2": hm(rw_g2[li]),
                "prm": jnp.concatenate([_head_minor_vec(p, nh_rw) for p in
                                        (rw_w0[li], rw_a0[li], rw_k_k[li], rw_k_a[li], rw_r_k[li])]
                                       + [jnp.zeros((3, d), F32)], axis=0),
                "ln": jnp.concatenate([_head_minor_vec(rw_ln_w[li], nh_rw), _head_minor_vec(rw_ln_b[li], nh_rw)], axis=0),
            }
            w_o = rw_w_o[li].reshape(nh_rw, RWKV_HEAD_DIM, d).swapaxes(0, 1).reshape(d, d).astype(BF16)
            hp = rmsnorm(yp, norm_mix[layer], F32)
            hs = rmsnorm(ys, norm_mix[layer], F32)
            zero_shift = jnp.zeros((bp, d), F32)
            zero_wkv = jnp.zeros((bp, nh_rw, RWKV_HEAD_DIM, RWKV_HEAD_DIM), F32)
            op, wkvp = rwkv_time_mix(hp, zero_shift, zero_wkv, wts, bp, tp)
            os_, wkvs = rwkv_time_mix(hs, state_rwkv_shift[li], state_rwkv_wkv[li], wts, bs, ts)
            yp = matmul(op, w_o, res=yp)
            ys = matmul(os_, w_o, res=ys)
            wkvp_l.append(wkvp)
            shp_l.append(hp.reshape(bp, tp, d)[:, -1, :])
            wkvs_l.append(wkvs)
            shs_l.append(hs.reshape(bs, ts, d)[:, -1, :])
        w1 = mlp_w1[layer].astype(BF16)
        w2 = mlp_w2[layer].astype(BF16)
        hid_p = matmul(rmsnorm(yp, norm_mlp[layer], BF16), w1, act="relu2", out_dtypes=(BF16,))
        yp = matmul(hid_p, w2, res=yp, tk=2048)
        hid_s = matmul(rmsnorm(ys, norm_mlp[layer], BF16), w1, act="relu2", out_dtypes=(BF16,))
        ys = matmul(hid_s, w2, res=ys, tk=2048)

    y_prompt = rmsnorm(yp, norm_final, F32).reshape(bp, tp, d)
    y_sample = rmsnorm(ys, norm_final, F32).reshape(bs, ts, d)
    return (y_prompt, y_sample, jnp.stack(kp_l), jnp.stack(vp_l), jnp.stack(ks_l), jnp.stack(vs_l),
            jnp.stack(wkvp_l), jnp.stack(shp_l), jnp.stack(wkvs_l), jnp.stack(shs_l))
```

```python
import functools

import jax
import jax.numpy as jnp
from jax import lax
from jax.experimental import pallas as pl
from jax.experimental.pallas import tpu as pltpu

F32 = jnp.float32
BF16 = jnp.bfloat16

LANES = 128
SUBLANES = 8
VMEM_LIMIT_BYTES = 56 * 1024 * 1024

PAGE_SIZE = 128
SB_HEAD_DIM = 128
RWKV_HEAD_DIM = 64
RMS_EPS = 1e-5
GN_EPS = 64e-5


def _params(*sem):
    return pltpu.CompilerParams(dimension_semantics=sem, vmem_limit_bytes=VMEM_LIMIT_BYTES)


def _rmsnorm_kernel(x_ref, g_ref, o_ref):
    x = x_ref[...]
    ms = jnp.mean(x * x, axis=-1, keepdims=True)
    o_ref[...] = ((x * lax.rsqrt(ms + RMS_EPS)) * g_ref[...]).astype(o_ref.dtype)


def rmsnorm(x, g, out_dtype, tm=256):
    m, d = x.shape
    tm = min(tm, m)
    assert m % tm == 0
    return pl.pallas_call(
        _rmsnorm_kernel,
        grid=(m // tm,),
        in_specs=[pl.BlockSpec((tm, d), lambda i: (i, 0)), pl.BlockSpec((1, d), lambda i: (0, 0))],
        out_specs=pl.BlockSpec((tm, d), lambda i: (i, 0)),
        out_shape=jax.ShapeDtypeStruct((m, d), out_dtype),
        compiler_params=_params("parallel"),
        name="rmsnorm",
    )(x, g.reshape(1, d).astype(F32))


def _mm_kernel(*refs, nk, act, has_res, n_out):
    x_ref, w_ref = refs[0], refs[1]
    res_ref = refs[2] if has_res else None
    first_out = 2 + int(has_res)
    out_refs = refs[first_out:first_out + n_out]
    acc_ref = refs[first_out + n_out] if nk > 1 else None

    def finish(acc):
        if act == "relu2":
            r = jnp.maximum(acc, 0.0)
            acc = r * r
        elif act == "tanh":
            acc = jnp.tanh(acc)
        elif act == "sigmoid":
            acc = 1.0 / (1.0 + jnp.exp(-acc))
        if has_res:
            acc = res_ref[...] + acc
        for o in out_refs:
            o[...] = acc.astype(o.dtype)

    part = jnp.dot(x_ref[...], w_ref[...], preferred_element_type=F32)
    if nk == 1:
        finish(part)
    else:
        k = pl.program_id(2)

        @pl.when(k == 0)
        def _():
            acc_ref[...] = part

        @pl.when(k > 0)
        def _():
            acc_ref[...] += part

        @pl.when(k == nk - 1)
        def _():
            finish(acc_ref[...])


def matmul(x, w, *, n_off=0, n=None, act=None, res=None, out_dtypes=(F32,), tm=1024, tn=512, tk=None):
    m, kdim = x.shape
    n = w.shape[1] if n is None else n
    tm, tn = min(tm, m), min(tn, n)
    tk = kdim if tk is None else min(tk, kdim)
    assert m % tm == 0 and n % tn == 0 and kdim % tk == 0 and n_off % tn == 0
    nk, joff = kdim // tk, n_off // tn
    in_specs = [pl.BlockSpec((tm, tk), lambda i, j, k: (i, k)),
                pl.BlockSpec((tk, tn), lambda i, j, k: (k, j + joff))]
    args = [x, w]
    if res is not None:
        in_specs.append(pl.BlockSpec((tm, tn), lambda i, j, k: (i, j)))
        args.append(res)
    out = pl.pallas_call(
        functools.partial(_mm_kernel, nk=nk, act=act, has_res=res is not None, n_out=len(out_dtypes)),
        grid=(m // tm, n // tn, nk),
        in_specs=in_specs,
        out_specs=[pl.BlockSpec((tm, tn), lambda i, j, k: (i, j)) for _ in out_dtypes],
        out_shape=[jax.ShapeDtypeStruct((m, n), dt) for dt in out_dtypes],
        scratch_shapes=[pltpu.VMEM((tm, tn), F32)] if nk > 1 else [],
        compiler_params=_params("parallel", "parallel", "arbitrary"),
        name="matmul",
    )(*args)
    return out[0] if len(out_dtypes) == 1 else tuple(out)


def _mm_f32w_kernel(*refs, act, has_res, n_out):
    x_ref, w_ref = refs[0], refs[1]
    res_ref = refs[2] if has_res else None
    first_out = 2 + int(has_res)
    out_refs = refs[first_out:first_out + n_out]
    wb_ref = refs[first_out + n_out]

    @pl.when(pl.program_id(1) == 0)
    def _():
        wb_ref[...] = w_ref[...].astype(BF16)

    acc = jnp.dot(x_ref[...], wb_ref[...], preferred_element_type=F32)
    if act == "relu2":
        r = jnp.maximum(acc, 0.0)
        acc = r * r
    if has_res:
        acc = res_ref[...] + acc
    for o in out_refs:
        o[...] = acc.astype(o.dtype)


def matmul_f32w(x, w, layer, *, n_off=0, n=None, act=None, res=None, out_dtypes=(F32,), tm=1024, tn=512):
    m, kdim = x.shape
    n = w.shape[2] if n is None else n
    tm, tn = min(tm, m), min(tn, n)
    assert m % tm == 0 and n % tn == 0 and n_off % tn == 0 and act in (None, "relu2") and w.shape[1] == kdim
    joff = n_off // tn
    in_specs = [pl.BlockSpec((tm, kdim), lambda j, i: (i, 0)),
                pl.BlockSpec((None, kdim, tn), lambda j, i: (layer, 0, j + joff))]
    args = [x, w]
    if res is not None:
        in_specs.append(pl.BlockSpec((tm, tn), lambda j, i: (i, j)))
        args.append(res)
    out = pl.pallas_call(
        functools.partial(_mm_f32w_kernel, act=act, has_res=res is not None, n_out=len(out_dtypes)),
        grid=(n // tn, m // tm),
        in_specs=in_specs,
        out_specs=[pl.BlockSpec((tm, tn), lambda j, i: (i, j)) for _ in out_dtypes],
        out_shape=[jax.ShapeDtypeStruct((m, n), dt) for dt in out_dtypes],
        scratch_shapes=[pltpu.VMEM((kdim, tn), BF16)],
        compiler_params=_params("parallel", "arbitrary"),
        name="matmul_f32w",
    )(*args)
    return out[0] if len(out_dtypes) == 1 else tuple(out)


def _log_sigmoid_pair(z):
    t = jnp.log(1.0 + jnp.exp(-jnp.abs(z)))
    return jnp.minimum(z, 0.0) - t, jnp.minimum(-z, 0.0) - t


def _suffix_sums(log_rest, u, two_terms=True):
    hi = log_rest.astype(BF16)
    if not two_terms:
        return jnp.dot(hi, u, preferred_element_type=F32)
    lo = (log_rest - hi.astype(F32)).astype(BF16)
    return (jnp.dot(hi, u, preferred_element_type=F32) + jnp.dot(lo, u, preferred_element_type=F32))


def _sb_prompt_kernel(bias_ref, q_ref, k_ref, v_ref, u_ref, o_ref, *, bq, hp, scale):
    hd = SB_HEAD_DIM
    h0 = pl.program_id(1) * hp
    qi = pl.program_id(2)
    u = u_ref[...]
    heads = range(hp)
    qs = [q_ref[:, hh * hd:(hh + 1) * hd] for hh in heads]
    biases = [bias_ref[h0 + hh] for hh in heads]
    n_chunks = bq // LANES

    def group(g, carries, accs, masked):
        start = pl.multiple_of(g * bq, bq)
        zs = []
        for hh in heads:
            k = k_ref[pl.ds(start, bq), hh * hd:(hh + 1) * hd]
            zs.append(lax.dot_general(qs[hh], k, (((1,), (1,)), ((), ())), preferred_element_type=F32)
                      * scale + biases[hh])
        if masked:
            row = lax.broadcasted_iota(jnp.int32, (bq, bq), 0)
            col = lax.broadcasted_iota(jnp.int32, (bq, bq), 1)
            mask = col < row
        ls_pos, ls_neg = [], []
        for hh in heads:
            lp, ln = _log_sigmoid_pair(zs[hh])
            ls_pos.append(lp)
            ls_neg.append(jnp.where(mask, ln, 0.0) if masked else ln)
        cs = [[_suffix_sums(ls_neg[hh][:, c * LANES:(c + 1) * LANES], u, two_terms=False)
               for c in range(n_chunks)] for hh in heads]
        new_carries, new_accs = [], []
        for hh in heads:
            carry = carries[hh]
            between = [None] * n_chunks
            for c in reversed(range(n_chunks)):
                between[c] = cs[hh][c][:, :LANES] + carry
                carry = carry + cs[hh][c][:, LANES:]
            p = jnp.exp(ls_pos[hh] + jnp.concatenate(between, axis=1))
            if masked:
                p = jnp.where(mask, p, 0.0)
            v = v_ref[pl.ds(start, bq), hh * hd:(hh + 1) * hd]
            new_accs.append(accs[hh] + jnp.dot(p.astype(BF16), v, preferred_element_type=F32))
            new_carries.append(carry)
        return new_carries, new_accs

    carries = [jnp.zeros((bq, LANES), F32) for _ in heads]
    accs = [jnp.zeros((bq, hd), F32) for _ in heads]
    carries, accs = group(qi, carries, accs, True)

    def body(i, ca):
        c, a = group(qi - 1 - i, list(ca[0]), list(ca[1]), False)
        return tuple(c), tuple(a)

    carries, accs = lax.fori_loop(0, qi, body, (tuple(carries), tuple(accs)))
    for hh in heads:
        o_ref[:, hh * hd:(hh + 1) * hd] = accs[hh].astype(o_ref.dtype)


def _suffix_matrix():
    j = lax.broadcasted_iota(jnp.int32, (LANES, 2 * LANES), 0)
    s = lax.broadcasted_iota(jnp.int32, (LANES, 2 * LANES), 1)
    return jnp.where((s >= LANES) | (j > s), 1.0, 0.0).astype(BF16)


def sb_attention_prompt(q, k, v, bias, b, t):
    n_heads = q.shape[1] // SB_HEAD_DIM
    hp = 4 if n_heads % 4 == 0 else 1
    bq = 256 if t % 256 == 0 else LANES
    assert t % bq == 0
    nq = t // bq
    wide = hp * SB_HEAD_DIM
    grid_spec = pltpu.PrefetchScalarGridSpec(
        num_scalar_prefetch=1,
        grid=(b, n_heads // hp, nq),
        in_specs=[pl.BlockSpec((bq, wide), lambda bi, h, i, _: (bi * nq + i, h)),
                  pl.BlockSpec((t, wide), lambda bi, h, i, _: (bi, h)),
                  pl.BlockSpec((t, wide), lambda bi, h, i, _: (bi, h)),
                  pl.BlockSpec((LANES, 2 * LANES), lambda bi, h, i, _: (0, 0))],
        out_specs=pl.BlockSpec((bq, wide), lambda bi, h, i, _: (bi * nq + i, h)),
    )
    return pl.pallas_call(
        functools.partial(_sb_prompt_kernel, bq=bq, hp=hp, scale=SB_HEAD_DIM ** -0.5),
        grid_spec=grid_spec,
        out_shape=jax.ShapeDtypeStruct(q.shape, BF16),
        compiler_params=_params("parallel", "parallel", "parallel"),
        name="sb_attention_prompt",
    )(bias.astype(F32), q, k, v, _suffix_matrix())


PAGES_PER_STEP = 2


def _sb_sample_kernel(pt_ref, qbd_ref, bias_ref, u_ref, knew_ref, vnew_ref, kc_hbm, vc_hbm,
                      o_ref, kbuf, vbuf, sem, carry_ref, acc_ref, *, li, n_pages, n_new, scale):
    b = pl.program_id(0)
    s = pl.program_id(1)
    n_seq = pl.num_programs(0)
    n_heads = qbd_ref.shape[0]
    steps = n_pages // PAGES_PER_STEP
    q_pos = n_pages * PAGE_SIZE
    qbd = qbd_ref[...]

    def page_copies(seq, step, slot):
        cps = []
        for j in range(PAGES_PER_STEP):
            page = pt_ref[seq, n_pages - PAGES_PER_STEP * step + (PAGES_PER_STEP - 1 - j)]
            for h in range(n_heads):
                win = pl.ds(h * SB_HEAD_DIM, SB_HEAD_DIM)
                cps.append(pltpu.make_async_copy(kc_hbm.at[li, page, :, h, :], kbuf.at[slot, j, :, win], sem.at[slot]))
                cps.append(pltpu.make_async_copy(vc_hbm.at[li, page, :, h, :], vbuf.at[slot, j, :, win], sem.at[slot]))
        return cps

    @pl.when((b == 0) & (s == 0))
    def _():
        for cp in page_copies(0, 1, 0):
            cp.start()

    @pl.when(s == 0)
    def _():
        carry_ref[...] = jnp.zeros_like(carry_ref)
        acc_ref[...] = jnp.zeros_like(acc_ref)

    def process(pages):
        zs = [lax.dot_general(qbd, k_page.astype(BF16), (((1,), (1,)), ((), ())), preferred_element_type=F32)
              * scale + bias_ref[...] for k_page, _, _, _ in pages]
        idx = lax.broadcasted_iota(jnp.int32, zs[0].shape, 1)
        masks = [((pos0 + idx) < q_pos) & (idx < n_valid) for _, _, pos0, n_valid in pages]
        ls = [_log_sigmoid_pair(z) for z in zs]
        cs = [_suffix_sums(jnp.where(m, ln, 0.0), u_ref[...]) for m, (_, ln) in zip(masks, ls)]
        carry = carry_ref[...]
        atts = []
        for m, (lp, _), c in zip(masks, ls, cs):
            atts.append(jnp.where(m, jnp.exp(lp + (c[:, :PAGE_SIZE] + carry)), 0.0).astype(BF16))
            carry = carry + c[:, PAGE_SIZE:]
        carry_ref[...] = carry
        acc = acc_ref[...]
        for att, (_, v_page, _, _) in zip(atts, pages):
            acc = acc + jnp.dot(att, v_page.astype(BF16), preferred_element_type=F32)
        acc_ref[...] = acc

    @pl.when(s == 0)
    def _():
        process([(knew_ref[...], vnew_ref[...], q_pos, n_new)])

    @pl.when(s > 0)
    def _():
        slot = (b * steps + s - 1) % 2

        @pl.when(s < steps)
        def _():
            for cp in page_copies(b, s + 1, 1 - slot):
                cp.start()

        @pl.when((s == steps) & (b + 1 < n_seq))
        def _():
            for cp in page_copies(b + 1, 1, 1 - slot):
                cp.start()

        for cp in page_copies(b, s, slot):
            cp.wait()
        first = (n_pages - PAGES_PER_STEP * s) * PAGE_SIZE
        process([(kbuf[slot, j], vbuf[slot, j], first + (PAGES_PER_STEP - 1 - j) * PAGE_SIZE, PAGE_SIZE)
                 for j in range(PAGES_PER_STEP)])

    @pl.when(s == steps)
    def _():
        acc = acc_ref[...]
        row = lax.broadcasted_iota(jnp.int32, acc.shape, 0)
        col = lax.broadcasted_iota(jnp.int32, acc.shape, 1)
        own = (col // SB_HEAD_DIM) == row
        o_ref[...] = jnp.sum(jnp.where(own, acc, 0.0), axis=0, keepdims=True)


def sb_attention_sample(q, k_new, v_new, cache_k, cache_v, li, page_table, bias):
    bs, d = q.shape
    n_heads = d // SB_HEAD_DIM
    n_pages = page_table.shape[1]
    assert n_pages % PAGES_PER_STEP == 0 and cache_k.shape[2:] == (PAGE_SIZE, n_heads, SB_HEAD_DIM)
    qh = q.reshape(bs, n_heads, 1, SB_HEAD_DIM)
    eye = jnp.eye(n_heads, dtype=F32).reshape(1, n_heads, n_heads, 1)
    qbd = (qh * eye).reshape(bs, n_heads, d).astype(BF16)
    pad = lambda a: jnp.pad(a.reshape(bs, 1, d), ((0, 0), (0, PAGE_SIZE - 1), (0, 0)))
    bias_b = jnp.broadcast_to(bias.astype(F32).reshape(n_heads, 1), (n_heads, PAGE_SIZE))
    page_buf = pltpu.VMEM((2, PAGES_PER_STEP, PAGE_SIZE, d), F32)
    grid_spec = pltpu.PrefetchScalarGridSpec(
        num_scalar_prefetch=1,
        grid=(bs, n_pages // PAGES_PER_STEP + 1),
        in_specs=[pl.BlockSpec((None, n_heads, d), lambda b, s, pt: (b, 0, 0)),
                  pl.BlockSpec((n_heads, PAGE_SIZE), lambda b, s, pt: (0, 0)),
                  pl.BlockSpec((LANES, 2 * LANES), lambda b, s, pt: (0, 0)),
                  pl.BlockSpec((None, PAGE_SIZE, d), lambda b, s, pt: (b, 0, 0)),
                  pl.BlockSpec((None, PAGE_SIZE, d), lambda b, s, pt: (b, 0, 0)),
                  pl.BlockSpec(memory_space=pl.ANY), pl.BlockSpec(memory_space=pl.ANY)],
        out_specs=pl.BlockSpec((None, 1, d), lambda b, s, pt: (b, 0, 0)),
        scratch_shapes=[page_buf, page_buf, pltpu.SemaphoreType.DMA((2,)),
                        pltpu.VMEM((n_heads, PAGE_SIZE), F32), pltpu.VMEM((n_heads, d), F32)],
    )
    out = pl.pallas_call(
        functools.partial(_sb_sample_kernel, li=li, n_pages=n_pages, n_new=1, scale=SB_HEAD_DIM ** -0.5),
        grid_spec=grid_spec,
        out_shape=jax.ShapeDtypeStruct((bs, 1, d), F32),
        compiler_params=_params("arbitrary", "arbitrary"),
        name="sb_attention_sample",
    )(page_table, qbd, bias_b, _suffix_matrix(), pad(k_new), pad(v_new), cache_k, cache_v)
    return out.reshape(bs, d)


def _head_minor_cols(w, n_heads):
    kdim = w.shape[0]
    return w.reshape(kdim, n_heads, RWKV_HEAD_DIM).swapaxes(1, 2).reshape(kdim, n_heads * RWKV_HEAD_DIM)


def _head_minor_vec(p, n_heads):
    return p.reshape(n_heads, RWKV_HEAD_DIM).T.reshape(1, n_heads * RWKV_HEAD_DIM).astype(F32)


def _mix_kernel(h_ref, xp_ref, mu_ref, *o_refs):
    h = h_ref[...]
    xx = xp_ref[...] - h
    for i, o in enumerate(o_refs):
        o[...] = (h + xx * mu_ref[i:i + 1, :]).astype(o.dtype)


def token_shift_mix(h, x_prev, mu, tm=128):
    m, d = h.shape
    tm = min(tm, m)
    assert m % tm == 0
    n_mix = mu.shape[0]
    row = pl.BlockSpec((tm, d), lambda i: (i, 0))
    return pl.pallas_call(
        _mix_kernel,
        grid=(m // tm,),
        in_specs=[row, row, pl.BlockSpec((n_mix, d), lambda i: (0, 0))],
        out_specs=[row] * n_mix,
        out_shape=[jax.ShapeDtypeStruct((m, d), BF16)] * n_mix,
        compiler_params=_params("parallel"),
        name="token_shift_mix",
    )(h, x_prev, mu.astype(F32))


def _head_sum(x):
    s = x[:, 0:LANES]
    for m in range(1, x.shape[1] // LANES):
        s = s + x[:, m * LANES:(m + 1) * LANES]
    return s + pltpu.roll(s, LANES // 2, 1)


def _tile_lanes(s, d):
    return jnp.concatenate([s] * (d // LANES), axis=1)


def _rwkv_prep_kernel(r_ref, k_ref, wl_ref, al_ref, prm_ref, decay_ref, k2_ref, kneg_ref, bb_ref, rk_ref):
    d = r_ref.shape[1]
    w0, a0, k_k, k_a, r_k = (prm_ref[i:i + 1, :] for i in range(5))
    u = -(w0 + wl_ref[...])
    softplus = jnp.maximum(u, 0.0) + jnp.log1p(jnp.exp(-jnp.abs(u)))
    decay_ref[...] = jnp.exp(-jnp.exp(-softplus - 0.5))
    a = 1.0 / (1.0 + jnp.exp(-(a0 + al_ref[...])))
    k = k_ref[...]
    kk = k * k_k
    norm = jnp.maximum(jnp.sqrt(_tile_lanes(_head_sum(kk * kk), d)), 1e-12)
    kk = kk / norm
    k2 = k * (1.0 + (a - 1.0) * k_a)
    k2_ref[...] = k2
    kneg_ref[...] = -kk
    bb_ref[...] = kk * a
    rk_ref[...] = _head_sum(r_ref[...] * k2 * r_k)


def rwkv_prep(r, k, wl, al, prm, tm=64):
    m, d = r.shape
    tm = min(tm, m)
    assert m % tm == 0
    row = pl.BlockSpec((tm, d), lambda i: (i, 0))
    full = jax.ShapeDtypeStruct((m, d), F32)
    return pl.pallas_call(
        _rwkv_prep_kernel,
        grid=(m // tm,),
        in_specs=[row, row, row, row, pl.BlockSpec(prm.shape, lambda i: (0, 0))],
        out_specs=[row, row, row, row, pl.BlockSpec((tm, LANES), lambda i: (i, 0))],
        out_shape=[full, full, full, full, jax.ShapeDtypeStruct((m, LANES), F32)],
        compiler_params=_params("parallel"),
        name="rwkv_prep",
    )(r, k, wl, al, prm)


N_PARTIAL_SUMS = 2


def _wkv_kernel(s0_ref, w_ref, k_ref, b_ref, a_ref, r_ref, v_ref, y_ref, st_ref, s_ref, *, tc, ns, nq, nm):
    c = pl.program_id(1)

    @pl.when(c == 0)
    def _():
        s_ref[...] = s0_ref[...]

    lane = lax.broadcasted_iota(jnp.int32, (SUBLANES, LANES), 1)
    lo_half = lane < (LANES // 2)
    swap = lambda x: pltpu.roll(x, LANES // 2, 1)
    rows = [(q, par) for q in range(nq) for par in range(2)]
    seqs = range(ns)

    def row(ref, e, t, m):
        return jnp.broadcast_to(ref[e, t, m:m + 1, :], (SUBLANES, LANES))

    def sum_partials(acc, key):
        tot = acc[key + (0,)]
        for i in range(1, min(N_PARTIAL_SUMS, nm)):
            tot = tot + acc[key + (i,)]
        return tot

    def step(t, _):
        sa_part = {}
        for e in seqs:
            for m in range(nm):
                a_m = row(a_ref, e, t, m)
                for qp in rows:
                    pr = s_ref[e, qp[0], qp[1], m] * a_m
                    key = (e, qp, m % N_PARTIAL_SUMS)
                    sa_part[key] = pr if key not in sa_part else sa_part[key] + pr
        sa, vt = {}, {}
        for e in seqs:
            for qp in rows:
                tot = sum_partials(sa_part, (e, qp))
                sa[(e, qp)] = tot + swap(tot)
            for q in range(nq):
                x = v_ref[e, t, q * SUBLANES:(q + 1) * SUBLANES, :]
                xs = swap(x)
                vt[(e, (q, 0))] = jnp.where(lo_half, x, xs)
                vt[(e, (q, 1))] = jnp.where(lo_half, xs, x)
        ya = {}
        for e in seqs:
            for m in range(nm):
                w_m, b_m, k_m, r_m = (row(ref, e, t, m) for ref in (w_ref, b_ref, k_ref, r_ref))
                for qp in rows:
                    s_new = s_ref[e, qp[0], qp[1], m] * w_m + sa[(e, qp)] * b_m + vt[(e, qp)] * k_m
                    s_ref[e, qp[0], qp[1], m] = s_new
                    pr = s_new * r_m
                    key = (e, qp, m % N_PARTIAL_SUMS)
                    ya[key] = pr if key not in ya else ya[key] + pr
        for e in seqs:
            for q in range(nq):
                ys = []
                for par in range(2):
                    tot = sum_partials(ya, (e, (q, par)))
                    ys.append(tot + swap(tot))
                y_ref[e, t, q * SUBLANES:(q + 1) * SUBLANES, :] = jnp.where(lo_half, ys[0], ys[1])
        return 0

    lax.fori_loop(0, tc, step, 0)

    @pl.when(c == pl.num_programs(1) - 1)
    def _():
        st_ref[...] = s_ref[...]


def wkv_scan(s0, w, k, b, a, r, v, nb, t):
    d = w.shape[1]
    nq, nm = s0.shape[1], s0.shape[3]
    ns = 2 if nb % 2 == 0 else 1
    tc = 32 if t % 32 == 0 else t
    assert t % tc == 0 and (tc % SUBLANES == 0 or tc == t)
    seq3 = lambda x: x.reshape(nb, t, d // LANES, LANES)
    col3 = pl.BlockSpec((ns, tc, d // LANES, LANES), lambda bi, c: (bi, c, 0, 0))
    st = pl.BlockSpec((ns,) + s0.shape[1:], lambda bi, c: (bi, 0, 0, 0, 0, 0))
    y, s_t = pl.pallas_call(
        functools.partial(_wkv_kernel, tc=tc, ns=ns, nq=nq, nm=nm),
        grid=(nb // ns, t // tc),
        in_specs=[st, col3, col3, col3, col3, col3, col3],
        out_specs=[col3, st],
        out_shape=[jax.ShapeDtypeStruct((nb, t, d // LANES, LANES), F32), jax.ShapeDtypeStruct(s0.shape, F32)],
        scratch_shapes=[pltpu.VMEM((ns,) + s0.shape[1:], F32)],
        compiler_params=_params("parallel", "arbitrary"),
        name="wkv_scan",
    )(s0, seq3(w), seq3(k), seq3(b), seq3(a), seq3(r), seq3(v))
    return y.reshape(nb * t, d), s_t


def _state_to_kernel(s):
    nb, nh, n, _ = s.shape
    x = s.reshape(nb, nh, n // 16, SUBLANES, 2, n // 2, 2)
    x = x.transpose(0, 2, 4, 5, 3, 6, 1)
    return x.reshape(nb, n // 16, 2, n // 2, SUBLANES, 2 * nh)


def _state_from_kernel(x, nh):
    nb, nq, _, nm, _, _ = x.shape
    x = x.reshape(nb, nq, 2, nm, SUBLANES, 2, nh)
    x = x.transpose(0, 6, 1, 4, 2, 3, 5)
    return x.reshape(nb, nh, nq * 16, nm * 2)


def _rwkv_post_kernel(y_ref, v_ref, g_ref, rk_ref, ln_ref, o_ref):
    d = y_ref.shape[1]
    n = RWKV_HEAD_DIM
    y = y_ref[...]
    mean = _tile_lanes(_head_sum(y), d) * (1.0 / n)
    yc = y - mean
    var = _tile_lanes(_head_sum(yc * yc), d) * (1.0 / n)
    yn = (yc * lax.rsqrt(var + GN_EPS)) * ln_ref[0:1, :] + ln_ref[1:2, :]
    bonus = _tile_lanes(rk_ref[...], d) * v_ref[...]
    o_ref[...] = ((yn + bonus) * g_ref[...]).astype(o_ref.dtype)


def rwkv_post(y, v, g, rk, ln, tm=128):
    m, d = y.shape
    tm = min(tm, m)
    assert m % tm == 0
    row = pl.BlockSpec((tm, d), lambda i: (i, 0))
    return pl.pallas_call(
        _rwkv_post_kernel,
        grid=(m // tm,),
        in_specs=[row, row, row, pl.BlockSpec((tm, LANES), lambda i: (i, 0)), pl.BlockSpec(ln.shape, lambda i: (0, 0))],
        out_specs=row,
        out_shape=jax.ShapeDtypeStruct((m, d), BF16),
        compiler_params=_params("parallel"),
        name="rwkv_post",
    )(y, v, g, rk, ln)


def rwkv_time_mix(h, shift_prev, s0, wts, nb, t):
    d = h.shape[1]
    nh = d // RWKV_HEAD_DIM
    h3 = h.reshape(nb, t, d)
    x_prev = jnp.concatenate([shift_prev[:, None, :].astype(F32), h3[:, :-1]], axis=1).reshape(nb * t, d)
    xr, xw, xk, xv, xa, xg = token_shift_mix(h, x_prev, wts["mu"])
    r = matmul(xr, wts["w_r"])
    k = matmul(xk, wts["w_k"])
    v = matmul(xv, wts["w_v"])
    wl = matmul(matmul(xw, wts["w1"], act="tanh", out_dtypes=(BF16,)), wts["w2"])
    al = matmul(matmul(xa, wts["a1"], out_dtypes=(BF16,)), wts["a2"])
    g = matmul(matmul(xg, wts["g1"], act="sigmoid", out_dtypes=(BF16,)), wts["g2"])
    decay, k2, kneg, bb, rk = rwkv_prep(r, k, wl, al, wts["prm"])
    y, s_t = wkv_scan(_state_to_kernel(s0.astype(F32)), decay, k2, bb, kneg, r, v, nb, t)
    out = rwkv_post(y, v, g, rk, wts["ln"])
    return out, _state_from_kernel(s_t, nh)


def kernel(x_prompt, x_sample, cache_sb_k, cache_sb_v, page_table, state_rwkv_wkv, state_rwkv_shift, norm_mix, norm_mlp, norm_final, sb_w_qkv, sb_w_o, sb_bias, rw_mu, rw_w_r, rw_w_k, rw_w_v, rw_w_o, rw_w0, rw_w1, rw_w2, rw_a0, rw_a1, rw_a2, rw_g1, rw_g2, rw_k_k, rw_k_a, rw_r_k, rw_ln_w, rw_ln_b, mlp_w1, mlp_w2):
    bp, tp, d = x_prompt.shape
    bs, ts, _ = x_sample.shape
    assert ts == 1, "the paged decode attention handles one new token per sequence"
    depth = norm_mix.shape[0]
    nh_rw = d // RWKV_HEAD_DIM
    n_sb_heads = d // SB_HEAD_DIM

    yp = x_prompt.reshape(bp * tp, d)
    ys = x_sample.reshape(bs * ts, d)
    kp_l, vp_l, ks_l, vs_l = [], [], [], []
    wkvp_l, shp_l, wkvs_l, shs_l = [], [], [], []

    for layer in range(depth):
        li = layer // 2
        if layer % 2 == 0:
            bias = sb_bias[li]
            hp = rmsnorm(yp, norm_mix[layer], BF16)
            hs = rmsnorm(ys, norm_mix[layer], BF16)
            qp = matmul_f32w(hp, sb_w_qkv, li, n_off=0, n=d, out_dtypes=(BF16,))
            kp, kp16 = matmul_f32w(hp, sb_w_qkv, li, n_off=d, n=d, out_dtypes=(F32, BF16))
            vp, vp16 = matmul_f32w(hp, sb_w_qkv, li, n_off=2 * d, n=d, out_dtypes=(F32, BF16))
            op = sb_attention_prompt(qp, kp16, vp16, bias, bp, tp)
            yp = matmul_f32w(op, sb_w_o, li, res=yp)
            qs = matmul_f32w(hs, sb_w_qkv, li, n_off=0, n=d)
            ks = matmul_f32w(hs, sb_w_qkv, li, n_off=d, n=d)
            vs = matmul_f32w(hs, sb_w_qkv, li, n_off=2 * d, n=d)
            os_ = sb_attention_sample(qs, ks, vs, cache_sb_k, cache_sb_v, li, page_table, bias)
            ys = matmul_f32w(os_.astype(BF16), sb_w_o, li, res=ys)
            kp_l.append(kp.reshape(bp, tp, n_sb_heads, SB_HEAD_DIM))
            vp_l.append(vp.reshape(bp, tp, n_sb_heads, SB_HEAD_DIM))
            ks_l.append(ks.reshape(bs, ts, n_sb_heads, SB_HEAD_DIM))
            vs_l.append(vs.reshape(bs, ts, n_sb_heads, SB_HEAD_DIM))
        else:
            hm = lambda w: _head_minor_cols(w, nh_rw).astype(BF16)
            wts = {
                "mu": rw_mu[li],
                "w_r": hm(rw_w_r[li]), "w_k": hm(rw_w_k[li]), "w_v": hm(rw_w_v[li]),
                "w1": rw_w1[li].astype(BF16), "w2": hm(rw_w2[li]),
                "a1": rw_a1[li].astype(BF16), "a2": hm(rw_a2[li]),
                "g1": rw_g1[li].astype(BF16), "g2": hm(rw_g2[li]),
                "prm": jnp.concatenate([_head_minor_vec(p, nh_rw) for p in
                                        (rw_w0[li], rw_a0[li], rw_k_k[li], rw_k_a[li], rw_r_k[li])]
                                       + [jnp.zeros((3, d), F32)], axis=0),
                "ln": jnp.concatenate([_head_minor_vec(rw_ln_w[li], nh_rw), _head_minor_vec(rw_ln_b[li], nh_rw)], axis=0),
            }
            w_o = rw_w_o[li].reshape(nh_rw, RWKV_HEAD_DIM, d).swapaxes(0, 1).reshape(d, d).astype(BF16)
            hp = rmsnorm(yp, norm_mix[layer], F32)
            hs = rmsnorm(ys, norm_mix[layer], F32)
            zero_shift = jnp.zeros((bp, d), F32)
            zero_wkv = jnp.zeros((bp, nh_rw, RWKV_HEAD_DIM, RWKV_HEAD_DIM), F32)
            op, wkvp = rwkv_time_mix(hp, zero_shift, zero_wkv, wts, bp, tp)
            os_, wkvs = rwkv_time_mix(hs, state_rwkv_shift[li], state_rwkv_wkv[li], wts, bs, ts)
            yp = matmul(op, w_o, res=yp)
            ys = matmul(os_, w_o, res=ys)
            wkvp_l.append(wkvp)
            shp_l.append(hp.reshape(bp, tp, d)[:, -1, :])
            wkvs_l.append(wkvs)
            shs_l.append(hs.reshape(bs, ts, d)[:, -1, :])
        w2 = mlp_w2[layer].astype(BF16)
        hid_p = matmul_f32w(rmsnorm(yp, norm_mlp[layer], BF16), mlp_w1, layer, act="relu2", out_dtypes=(BF16,))
        yp = matmul(hid_p, w2, res=yp, tk=4096)
        hid_s = matmul_f32w(rmsnorm(ys, norm_mlp[layer], BF16), mlp_w1, layer, act="relu2", out_dtypes=(BF16,))
        ys = matmul(hid_s, w2, res=ys, tk=4096)

    y_prompt = rmsnorm(yp, norm_final, F32).reshape(bp, tp, d)
    y_sample = rmsnorm(ys, norm_final, F32).reshape(bs, ts, d)
    return (y_prompt, y_sample, jnp.stack(kp_l), jnp.stack(vp_l), jnp.stack(ks_l), jnp.stack(vs_l),
            jnp.stack(wkvp_l), jnp.stack(shp_l), jnp.stack(wkvs_l), jnp.stack(shs_l))
```

```python
import functools

import jax
import jax.numpy as jnp
from jax import lax
from jax.experimental import pallas as pl
from jax.experimental.pallas import tpu as pltpu

F32 = jnp.float32
BF16 = jnp.bfloat16

LANES = 128
SUBLANES = 8
VMEM_LIMIT_BYTES = 56 * 1024 * 1024

PAGE_SIZE = 128
SB_HEAD_DIM = 128
RWKV_HEAD_DIM = 64
RMS_EPS = 1e-5
GN_EPS = 64e-5


def _params(*sem):
    return pltpu.CompilerParams(dimension_semantics=sem, vmem_limit_bytes=VMEM_LIMIT_BYTES)


def _rmsnorm_kernel(x_ref, g_ref, o_ref):
    x = x_ref[...]
    ms = jnp.mean(x * x, axis=-1, keepdims=True)
    o_ref[...] = ((x * lax.rsqrt(ms + RMS_EPS)) * g_ref[...]).astype(o_ref.dtype)


def rmsnorm(x, g, out_dtype, tm=256):
    m, d = x.shape
    tm = min(tm, m)
    assert m % tm == 0
    return pl.pallas_call(
        _rmsnorm_kernel,
        grid=(m // tm,),
        in_specs=[pl.BlockSpec((tm, d), lambda i: (i, 0)), pl.BlockSpec((1, d), lambda i: (0, 0))],
        out_specs=pl.BlockSpec((tm, d), lambda i: (i, 0)),
        out_shape=jax.ShapeDtypeStruct((m, d), out_dtype),
        compiler_params=_params("parallel"),
        name="rmsnorm",
    )(x, g.reshape(1, d).astype(F32))


def _mm_kernel(*refs, nk, act, has_res, n_out, has_s, w_f32):
    it = iter(refs)
    x_ref, w_ref = next(it), next(it)
    res_ref = next(it) if has_res else None
    xs_ref = next(it) if has_s else None
    ress_ref = next(it) if (has_s and has_res) else None
    out_refs = [next(it) for _ in range(n_out)]
    outs_refs = [next(it) for _ in range(n_out)] if has_s else []
    wb_ref = next(it) if w_f32 else None
    acc_ref = next(it) if nk > 1 else None
    accs_ref = next(it) if (nk > 1 and has_s) else None
    i, k = pl.program_id(1), pl.program_id(2)

    if w_f32:
        @pl.when(i == 0)
        def _():
            wb_ref[...] = w_ref[...].astype(BF16)
        w_ref = wb_ref

    def finish(acc, r_ref, o_refs):
        if act == "relu2":
            r = jnp.maximum(acc, 0.0)
            acc = r * r
        elif act == "tanh":
            acc = jnp.tanh(acc)
        elif act == "sigmoid":
            acc = 1.0 / (1.0 + jnp.exp(-acc))
        if has_res:
            acc = r_ref[...] + acc
        for o in o_refs:
            o[...] = acc.astype(o.dtype)

    def run(lhs_ref, r_ref, o_refs, a_ref):
        part = jnp.dot(lhs_ref[...], w_ref[...], preferred_element_type=F32)
        if nk == 1:
            finish(part, r_ref, o_refs)
            return

        @pl.when(k == 0)
        def _():
            a_ref[...] = part

        @pl.when(k > 0)
        def _():
            a_ref[...] += part

        @pl.when(k == nk - 1)
        def _():
            finish(a_ref[...], r_ref, o_refs)

    run(x_ref, res_ref, out_refs, acc_ref)
    if has_s:
        @pl.when(i == 0)
        def _():
            run(xs_ref, ress_ref, outs_refs, accs_ref)


def matmul(x, w, *, layer=None, n_off=0, n=None, act=None, res=None, xs=None, res_s=None,
           out_dtypes=(F32,), tm=1024, tn=512, tk=None):
    m, kdim = x.shape
    w_f32 = layer is not None
    n = w.shape[-1] if n is None else n
    tm, tn = min(tm, m), min(tn, n)
    tk = kdim if tk is None else min(tk, kdim)
    assert m % tm == 0 and n % tn == 0 and kdim % tk == 0 and n_off % tn == 0 and w.shape[-2] == kdim
    assert not (w_f32 and tk != kdim) and (w_f32 or w.dtype == BF16)
    assert (res_s is None) == (res is None or xs is None)
    nk, joff = kdim // tk, n_off // tn
    has_s, has_res = xs is not None, res is not None
    if w_f32:
        w_spec = pl.BlockSpec((None, tk, tn), lambda j, i, k: (layer, k, j + joff))
    else:
        w_spec = pl.BlockSpec((tk, tn), lambda j, i, k: (k, j + joff))
    in_specs, args = [pl.BlockSpec((tm, tk), lambda j, i, k: (i, k)), w_spec], [x, w]
    if has_res:
        in_specs.append(pl.BlockSpec((tm, tn), lambda j, i, k: (i, j)))
        args.append(res)
    out_specs = [pl.BlockSpec((tm, tn), lambda j, i, k: (i, j)) for _ in out_dtypes]
    out_shape = [jax.ShapeDtypeStruct((m, n), dt) for dt in out_dtypes]
    scratch = [pltpu.VMEM((tk, tn), BF16)] if w_f32 else []
    if nk > 1:
        scratch.append(pltpu.VMEM((tm, tn), F32))
    if has_s:
        ms = xs.shape[0]
        in_specs.append(pl.BlockSpec((ms, tk), lambda j, i, k: (0, k)))
        args.append(xs)
        if has_res:
            in_specs.append(pl.BlockSpec((ms, tn), lambda j, i, k: (0, j)))
            args.append(res_s)
        out_specs += [pl.BlockSpec((ms, tn), lambda j, i, k: (0, j)) for _ in out_dtypes]
        out_shape += [jax.ShapeDtypeStruct((ms, n), dt) for dt in out_dtypes]
        if nk > 1:
            scratch.append(pltpu.VMEM((ms, tn), F32))
    out = pl.pallas_call(
        functools.partial(_mm_kernel, nk=nk, act=act, has_res=has_res, n_out=len(out_dtypes), has_s=has_s,
                          w_f32=w_f32),
        grid=(n // tn, m // tm, nk),
        in_specs=in_specs,
        out_specs=out_specs,
        out_shape=out_shape,
        scratch_shapes=scratch,
        compiler_params=_params("parallel", "arbitrary", "arbitrary"),
        name="matmul_f32w" if w_f32 else "matmul",
    )(*args)
    unpack = lambda o: o[0] if len(out_dtypes) == 1 else tuple(o)
    if has_s:
        return unpack(out[:len(out_dtypes)]), unpack(out[len(out_dtypes):])
    return unpack(out)


def _log_sigmoid_pair(z):
    t = jnp.log(1.0 + jnp.exp(-jnp.abs(z)))
    return jnp.minimum(z, 0.0) - t, jnp.minimum(-z, 0.0) - t


def _suffix_sums(log_rest, u, two_terms=True):
    hi = log_rest.astype(BF16)
    if not two_terms:
        return jnp.dot(hi, u, preferred_element_type=F32)
    lo = (log_rest - hi.astype(F32)).astype(BF16)
    return (jnp.dot(hi, u, preferred_element_type=F32) + jnp.dot(lo, u, preferred_element_type=F32))


def _sb_prompt_kernel(bias_ref, q_ref, k_ref, v_ref, u_ref, o_ref, *, bq, hp, scale):
    hd = SB_HEAD_DIM
    h0 = pl.program_id(1) * hp
    qi = pl.program_id(2)
    u = u_ref[...]
    heads = range(hp)
    qs = [q_ref[:, hh * hd:(hh + 1) * hd] for hh in heads]
    biases = [bias_ref[h0 + hh] for hh in heads]
    n_chunks = bq // LANES

    def group(g, carries, accs, masked):
        start = pl.multiple_of(g * bq, bq)
        zs = []
        for hh in heads:
            k = k_ref[pl.ds(start, bq), hh * hd:(hh + 1) * hd]
            zs.append(lax.dot_general(qs[hh], k, (((1,), (1,)), ((), ())), preferred_element_type=F32)
                      * scale + biases[hh])
        if masked:
            row = lax.broadcasted_iota(jnp.int32, (bq, bq), 0)
            col = lax.broadcasted_iota(jnp.int32, (bq, bq), 1)
            mask = col < row
        ls_pos, ls_neg = [], []
        for hh in heads:
            lp, ln = _log_sigmoid_pair(zs[hh])
            ls_pos.append(lp)
            ls_neg.append(jnp.where(mask, ln, 0.0) if masked else ln)
        cs = [[_suffix_sums(ls_neg[hh][:, c * LANES:(c + 1) * LANES], u, two_terms=False)
               for c in range(n_chunks)] for hh in heads]
        new_carries, new_accs = [], []
        for hh in heads:
            carry = carries[hh]
            between = [None] * n_chunks
            for c in reversed(range(n_chunks)):
                between[c] = cs[hh][c][:, :LANES] + carry
                carry = carry + cs[hh][c][:, LANES:]
            p = jnp.exp(ls_pos[hh] + jnp.concatenate(between, axis=1))
            if masked:
                p = jnp.where(mask, p, 0.0)
            v = v_ref[pl.ds(start, bq), hh * hd:(hh + 1) * hd]
            new_accs.append(accs[hh] + jnp.dot(p.astype(BF16), v, preferred_element_type=F32))
            new_carries.append(carry)
        return new_carries, new_accs

    carries = [jnp.zeros((bq, LANES), F32) for _ in heads]
    accs = [jnp.zeros((bq, hd), F32) for _ in heads]
    carries, accs = group(qi, carries, accs, True)

    def body(i, ca):
        c, a = group(qi - 1 - i, list(ca[0]), list(ca[1]), False)
        return tuple(c), tuple(a)

    carries, accs = lax.fori_loop(0, qi, body, (tuple(carries), tuple(accs)))
    for hh in heads:
        o_ref[:, hh * hd:(hh + 1) * hd] = accs[hh].astype(o_ref.dtype)


def _suffix_matrix():
    j = lax.broadcasted_iota(jnp.int32, (LANES, 2 * LANES), 0)
    s = lax.broadcasted_iota(jnp.int32, (LANES, 2 * LANES), 1)
    return jnp.where((s >= LANES) | (j > s), 1.0, 0.0).astype(BF16)


def sb_attention_prompt(q, k, v, bias, b, t):
    n_heads = q.shape[1] // SB_HEAD_DIM
    hp = 4 if n_heads % 4 == 0 else 1
    bq = 256 if t % 256 == 0 else LANES
    assert t % bq == 0
    nq = t // bq
    wide = hp * SB_HEAD_DIM
    grid_spec = pltpu.PrefetchScalarGridSpec(
        num_scalar_prefetch=1,
        grid=(b, n_heads // hp, nq),
        in_specs=[pl.BlockSpec((bq, wide), lambda bi, h, i, _: (bi * nq + i, h)),
                  pl.BlockSpec((t, wide), lambda bi, h, i, _: (bi, h)),
                  pl.BlockSpec((t, wide), lambda bi, h, i, _: (bi, h)),
                  pl.BlockSpec((LANES, 2 * LANES), lambda bi, h, i, _: (0, 0))],
        out_specs=pl.BlockSpec((bq, wide), lambda bi, h, i, _: (bi * nq + i, h)),
    )
    return pl.pallas_call(
        functools.partial(_sb_prompt_kernel, bq=bq, hp=hp, scale=SB_HEAD_DIM ** -0.5),
        grid_spec=grid_spec,
        out_shape=jax.ShapeDtypeStruct(q.shape, BF16),
        compiler_params=_params("parallel", "parallel", "parallel"),
        name="sb_attention_prompt",
    )(bias.astype(F32), q, k, v, _suffix_matrix())


PAGES_PER_STEP = 2


def _sb_sample_kernel(pt_ref, qbd_ref, bias_ref, u_ref, knew_ref, vnew_ref, kc_hbm, vc_hbm,
                      o_ref, kbuf, vbuf, sem, carry_ref, acc_ref, *, li, n_pages, n_new, scale):
    b = pl.program_id(0)
    s = pl.program_id(1)
    n_seq = pl.num_programs(0)
    n_heads = qbd_ref.shape[0]
    steps = n_pages // PAGES_PER_STEP
    q_pos = n_pages * PAGE_SIZE
    qbd = qbd_ref[...]

    def page_copies(seq, step, slot):
        cps = []
        for j in range(PAGES_PER_STEP):
            page = pt_ref[seq, n_pages - PAGES_PER_STEP * step + (PAGES_PER_STEP - 1 - j)]
            for h in range(n_heads):
                win = pl.ds(h * SB_HEAD_DIM, SB_HEAD_DIM)
                cps.append(pltpu.make_async_copy(kc_hbm.at[li, page, :, h, :], kbuf.at[slot, j, :, win], sem.at[slot]))
                cps.append(pltpu.make_async_copy(vc_hbm.at[li, page, :, h, :], vbuf.at[slot, j, :, win], sem.at[slot]))
        return cps

    @pl.when((b == 0) & (s == 0))
    def _():
        for cp in page_copies(0, 1, 0):
            cp.start()

    @pl.when(s == 0)
    def _():
        carry_ref[...] = jnp.zeros_like(carry_ref)
        acc_ref[...] = jnp.zeros_like(acc_ref)

    def process(pages):
        zs = [lax.dot_general(qbd, k_page.astype(BF16), (((1,), (1,)), ((), ())), preferred_element_type=F32)
              * scale + bias_ref[...] for k_page, _, _, _ in pages]
        idx = lax.broadcasted_iota(jnp.int32, zs[0].shape, 1)
        masks = [((pos0 + idx) < q_pos) & (idx < n_valid) for _, _, pos0, n_valid in pages]
        ls = [_log_sigmoid_pair(z) for z in zs]
        cs = [_suffix_sums(jnp.where(m, ln, 0.0), u_ref[...]) for m, (_, ln) in zip(masks, ls)]
        carry = carry_ref[...]
        atts = []
        for m, (lp, _), c in zip(masks, ls, cs):
            atts.append(jnp.where(m, jnp.exp(lp + (c[:, :PAGE_SIZE] + carry)), 0.0).astype(BF16))
            carry = carry + c[:, PAGE_SIZE:]
        carry_ref[...] = carry
        acc = acc_ref[...]
        for att, (_, v_page, _, _) in zip(atts, pages):
            acc = acc + jnp.dot(att, v_page.astype(BF16), preferred_element_type=F32)
        acc_ref[...] = acc

    @pl.when(s == 0)
    def _():
        first_row = lax.broadcasted_iota(jnp.int32, (PAGE_SIZE, qbd.shape[1]), 0) < n_new
        process([(jnp.where(first_row, knew_ref[...], 0.0), jnp.where(first_row, vnew_ref[...], 0.0), q_pos, n_new)])

    @pl.when(s > 0)
    def _():
        slot = (b * steps + s - 1) % 2

        @pl.when(s < steps)
        def _():
            for cp in page_copies(b, s + 1, 1 - slot):
                cp.start()

        @pl.when((s == steps) & (b + 1 < n_seq))
        def _():
            for cp in page_copies(b + 1, 1, 1 - slot):
                cp.start()

        for cp in page_copies(b, s, slot):
            cp.wait()
        first = (n_pages - PAGES_PER_STEP * s) * PAGE_SIZE
        process([(kbuf[slot, j], vbuf[slot, j], first + (PAGES_PER_STEP - 1 - j) * PAGE_SIZE, PAGE_SIZE)
                 for j in range(PAGES_PER_STEP)])

    @pl.when(s == steps)
    def _():
        acc = acc_ref[...]
        row = lax.broadcasted_iota(jnp.int32, acc.shape, 0)
        col = lax.broadcasted_iota(jnp.int32, acc.shape, 1)
        own = (col // SB_HEAD_DIM) == row
        o_ref[...] = jnp.sum(jnp.where(own, acc, 0.0), axis=0, keepdims=True)


def sb_attention_sample(q, k_new, v_new, cache_k, cache_v, li, page_table, bias):
    bs, d = q.shape
    n_heads = d // SB_HEAD_DIM
    n_pages = page_table.shape[1]
    assert n_pages % PAGES_PER_STEP == 0 and cache_k.shape[2:] == (PAGE_SIZE, n_heads, SB_HEAD_DIM)
    qh = q.reshape(bs, n_heads, 1, SB_HEAD_DIM)
    eye = jnp.eye(n_heads, dtype=F32).reshape(1, n_heads, n_heads, 1)
    qbd = (qh * eye).reshape(bs, n_heads, d).astype(BF16)
    bias_b = jnp.broadcast_to(bias.astype(F32).reshape(n_heads, 1), (n_heads, PAGE_SIZE))
    page_buf = pltpu.VMEM((2, PAGES_PER_STEP, PAGE_SIZE, d), F32)
    grid_spec = pltpu.PrefetchScalarGridSpec(
        num_scalar_prefetch=1,
        grid=(bs, n_pages // PAGES_PER_STEP + 1),
        in_specs=[pl.BlockSpec((None, n_heads, d), lambda b, s, pt: (b, 0, 0)),
                  pl.BlockSpec((n_heads, PAGE_SIZE), lambda b, s, pt: (0, 0)),
                  pl.BlockSpec((LANES, 2 * LANES), lambda b, s, pt: (0, 0)),
                  pl.BlockSpec((None, 1, d), lambda b, s, pt: (b, 0, 0)),
                  pl.BlockSpec((None, 1, d), lambda b, s, pt: (b, 0, 0)),
                  pl.BlockSpec(memory_space=pl.ANY), pl.BlockSpec(memory_space=pl.ANY)],
        out_specs=pl.BlockSpec((None, 1, d), lambda b, s, pt: (b, 0, 0)),
        scratch_shapes=[page_buf, page_buf, pltpu.SemaphoreType.DMA((2,)),
                        pltpu.VMEM((n_heads, PAGE_SIZE), F32), pltpu.VMEM((n_heads, d), F32)],
    )
    out = pl.pallas_call(
        functools.partial(_sb_sample_kernel, li=li, n_pages=n_pages, n_new=1, scale=SB_HEAD_DIM ** -0.5),
        grid_spec=grid_spec,
        out_shape=jax.ShapeDtypeStruct((bs, 1, d), F32),
        compiler_params=_params("arbitrary", "arbitrary"),
        name="sb_attention_sample",
    )(page_table, qbd, bias_b, _suffix_matrix(), k_new.reshape(bs, 1, d), v_new.reshape(bs, 1, d), cache_k, cache_v)
    return out.reshape(bs, d)


def _head_minor_cols(w, n_heads):
    kdim = w.shape[0]
    return w.reshape(kdim, n_heads, RWKV_HEAD_DIM).swapaxes(1, 2).reshape(kdim, n_heads * RWKV_HEAD_DIM)


def _head_minor_vec(p, n_heads):
    return p.reshape(n_heads, RWKV_HEAD_DIM).T.reshape(1, n_heads * RWKV_HEAD_DIM).astype(F32)


def _mix_kernel(h_ref, xp_ref, mu_ref, *o_refs):
    h = h_ref[...]
    xx = xp_ref[...] - h
    for i, o in enumerate(o_refs):
        o[...] = (h + xx * mu_ref[i:i + 1, :]).astype(o.dtype)


def token_shift_mix(h, x_prev, mu, tm=128):
    m, d = h.shape
    tm = min(tm, m)
    assert m % tm == 0
    n_mix = mu.shape[0]
    row = pl.BlockSpec((tm, d), lambda i: (i, 0))
    return pl.pallas_call(
        _mix_kernel,
        grid=(m // tm,),
        in_specs=[row, row, pl.BlockSpec((n_mix, d), lambda i: (0, 0))],
        out_specs=[row] * n_mix,
        out_shape=[jax.ShapeDtypeStruct((m, d), BF16)] * n_mix,
        compiler_params=_params("parallel"),
        name="token_shift_mix",
    )(h, x_prev, mu.astype(F32))


def _head_sum(x):
    s = x[:, 0:LANES]
    for m in range(1, x.shape[1] // LANES):
        s = s + x[:, m * LANES:(m + 1) * LANES]
    return s + pltpu.roll(s, LANES // 2, 1)


def _tile_lanes(s, d):
    return jnp.concatenate([s] * (d // LANES), axis=1)


def _rwkv_prep_kernel(r_ref, k_ref, tw_ref, ta_ref, w2_ref, a2_ref, prm_ref,
                      decay_ref, k2_ref, kneg_ref, bb_ref, rk_ref):
    d = r_ref.shape[1]
    w0, a0, k_k, k_a, r_k = (prm_ref[i:i + 1, :] for i in range(5))
    u = -(w0 + jnp.dot(tw_ref[...], w2_ref[...], preferred_element_type=F32))
    softplus = jnp.maximum(u, 0.0) + jnp.log1p(jnp.exp(-jnp.abs(u)))
    decay_ref[...] = jnp.exp(-jnp.exp(-softplus - 0.5))
    a = 1.0 / (1.0 + jnp.exp(-(a0 + jnp.dot(ta_ref[...], a2_ref[...], preferred_element_type=F32))))
    k = k_ref[...]
    kk = k * k_k
    norm = jnp.maximum(jnp.sqrt(_tile_lanes(_head_sum(kk * kk), d)), 1e-12)
    kk = kk / norm
    k2 = k * (1.0 + (a - 1.0) * k_a)
    k2_ref[...] = k2
    kneg_ref[...] = -kk
    bb_ref[...] = kk * a
    rk_ref[...] = _head_sum(r_ref[...] * k2 * r_k)


def rwkv_prep(r, k, tw, ta, w2, a2, prm, tm=64):
    m, d = r.shape
    tm = min(tm, m)
    assert m % tm == 0
    row = pl.BlockSpec((tm, d), lambda i: (i, 0))
    lora = lambda x: pl.BlockSpec((tm, x.shape[1]), lambda i: (i, 0))
    whole = lambda x: pl.BlockSpec(x.shape, lambda i: (0, 0))
    full = jax.ShapeDtypeStruct((m, d), F32)
    return pl.pallas_call(
        _rwkv_prep_kernel,
        grid=(m // tm,),
        in_specs=[row, row, lora(tw), lora(ta), whole(w2), whole(a2), whole(prm)],
        out_specs=[row, row, row, row, pl.BlockSpec((tm, LANES), lambda i: (i, 0))],
        out_shape=[full, full, full, full, jax.ShapeDtypeStruct((m, LANES), F32)],
        compiler_params=_params("parallel"),
        name="rwkv_prep",
    )(r, k, tw, ta, w2, a2, prm)


N_PARTIAL_SUMS = 2


def _wkv_kernel(s0_ref, w_ref, k_ref, b_ref, a_ref, r_ref, v_ref, y_ref, st_ref, s_ref, *, tc, ns, nq, nm):
    c = pl.program_id(1)

    @pl.when(c == 0)
    def _():
        s_ref[...] = s0_ref[...]

    lane = lax.broadcasted_iota(jnp.int32, (SUBLANES, LANES), 1)
    lo_half = lane < (LANES // 2)
    swap = lambda x: pltpu.roll(x, LANES // 2, 1)
    rows = [(q, par) for q in range(nq) for par in range(2)]
    seqs = range(ns)

    def row(ref, e, t, m):
        return jnp.broadcast_to(ref[e, t, m:m + 1, :], (SUBLANES, LANES))

    def sum_partials(acc, key):
        tot = acc[key + (0,)]
        for i in range(1, min(N_PARTIAL_SUMS, nm)):
            tot = tot + acc[key + (i,)]
        return tot

    def step(t, _):
        sa_part = {}
        for e in seqs:
            for m in range(nm):
                a_m = row(a_ref, e, t, m)
                for qp in rows:
                    pr = s_ref[e, qp[0], qp[1], m] * a_m
                    key = (e, qp, m % N_PARTIAL_SUMS)
                    sa_part[key] = pr if key not in sa_part else sa_part[key] + pr
        sa, vt = {}, {}
        for e in seqs:
            for qp in rows:
                tot = sum_partials(sa_part, (e, qp))
                sa[(e, qp)] = tot + swap(tot)
            for q in range(nq):
                x = v_ref[e, t, q * SUBLANES:(q + 1) * SUBLANES, :]
                xs = swap(x)
                vt[(e, (q, 0))] = jnp.where(lo_half, x, xs)
                vt[(e, (q, 1))] = jnp.where(lo_half, xs, x)
        ya = {}
        for e in seqs:
            for m in range(nm):
                w_m, b_m, k_m, r_m = (row(ref, e, t, m) for ref in (w_ref, b_ref, k_ref, r_ref))
                for qp in rows:
                    s_new = s_ref[e, qp[0], qp[1], m] * w_m + sa[(e, qp)] * b_m + vt[(e, qp)] * k_m
                    s_ref[e, qp[0], qp[1], m] = s_new
                    pr = s_new * r_m
                    key = (e, qp, m % N_PARTIAL_SUMS)
                    ya[key] = pr if key not in ya else ya[key] + pr
        for e in seqs:
            for q in range(nq):
                ys = []
                for par in range(2):
                    tot = sum_partials(ya, (e, (q, par)))
                    ys.append(tot + swap(tot))
                y_ref[e, t, q * SUBLANES:(q + 1) * SUBLANES, :] = jnp.where(lo_half, ys[0], ys[1])
        return 0

    lax.fori_loop(0, tc, step, 0)

    @pl.when(c == pl.num_programs(1) - 1)
    def _():
        st_ref[...] = s_ref[...]


def wkv_scan(s0, w, k, b, a, r, v, nb, t):
    d = w.shape[1]
    nq, nm = s0.shape[1], s0.shape[3]
    ns = 2 if nb % 2 == 0 else 1
    tc = 32 if t % 32 == 0 else t
    assert t % tc == 0 and (tc % SUBLANES == 0 or tc == t)
    seq3 = lambda x: x.reshape(nb, t, d // LANES, LANES)
    col3 = pl.BlockSpec((ns, tc, d // LANES, LANES), lambda bi, c: (bi, c, 0, 0))
    st = pl.BlockSpec((ns,) + s0.shape[1:], lambda bi, c: (bi, 0, 0, 0, 0, 0))
    y, s_t = pl.pallas_call(
        functools.partial(_wkv_kernel, tc=tc, ns=ns, nq=nq, nm=nm),
        grid=(nb // ns, t // tc),
        in_specs=[st, col3, col3, col3, col3, col3, col3],
        out_specs=[col3, st],
        out_shape=[jax.ShapeDtypeStruct((nb, t, d // LANES, LANES), F32), jax.ShapeDtypeStruct(s0.shape, F32)],
        scratch_shapes=[pltpu.VMEM((ns,) + s0.shape[1:], F32)],
        compiler_params=_params("parallel", "arbitrary"),
        name="wkv_scan",
    )(s0, seq3(w), seq3(k), seq3(b), seq3(a), seq3(r), seq3(v))
    return y.reshape(nb * t, d), s_t


def _state_to_kernel(s):
    nb, nh, n, _ = s.shape
    x = s.reshape(nb, nh, n // 16, SUBLANES, 2, n // 2, 2)
    x = x.transpose(0, 2, 4, 5, 3, 6, 1)
    return x.reshape(nb, n // 16, 2, n // 2, SUBLANES, 2 * nh)


def _state_from_kernel(x, nh):
    nb, nq, _, nm, _, _ = x.shape
    x = x.reshape(nb, nq, 2, nm, SUBLANES, 2, nh)
    x = x.transpose(0, 6, 1, 4, 2, 3, 5)
    return x.reshape(nb, nh, nq * 16, nm * 2)


def _rwkv_post_kernel(y_ref, v_ref, sg_ref, g2_ref, rk_ref, ln_ref, o_ref):
    d = y_ref.shape[1]
    gate = jnp.dot(sg_ref[...], g2_ref[...], preferred_element_type=F32)
    n = RWKV_HEAD_DIM
    y = y_ref[...]
    mean = _tile_lanes(_head_sum(y), d) * (1.0 / n)
    yc = y - mean
    var = _tile_lanes(_head_sum(yc * yc), d) * (1.0 / n)
    yn = (yc * lax.rsqrt(var + GN_EPS)) * ln_ref[0:1, :] + ln_ref[1:2, :]
    bonus = _tile_lanes(rk_ref[...], d) * v_ref[...]
    o_ref[...] = ((yn + bonus) * gate).astype(o_ref.dtype)


def rwkv_post(y, v, sg, g2, rk, ln, tm=128):
    m, d = y.shape
    tm = min(tm, m)
    assert m % tm == 0
    row = pl.BlockSpec((tm, d), lambda i: (i, 0))
    return pl.pallas_call(
        _rwkv_post_kernel,
        grid=(m // tm,),
        in_specs=[row, row, pl.BlockSpec((tm, sg.shape[1]), lambda i: (i, 0)), pl.BlockSpec(g2.shape, lambda i: (0, 0)),
                  pl.BlockSpec((tm, LANES), lambda i: (i, 0)), pl.BlockSpec(ln.shape, lambda i: (0, 0))],
        out_specs=row,
        out_shape=jax.ShapeDtypeStruct((m, d), BF16),
        compiler_params=_params("parallel"),
        name="rwkv_post",
    )(y, v, sg, g2, rk, ln)


def rwkv_time_mix(streams, wts):
    mixes = []
    for h, shift_prev, _, nb, t in streams:
        d = h.shape[1]
        h3 = h.reshape(nb, t, d)
        x_prev = jnp.concatenate([shift_prev[:, None, :].astype(F32), h3[:, :-1]], axis=1).reshape(nb * t, d)
        mixes.append(token_shift_mix(h, x_prev, wts["mu"]))
    (xr, xw, xk, xv, xa, xg), (xr_s, xw_s, xk_s, xv_s, xa_s, xg_s) = mixes
    proj = lambda x, x_s, w, **kw: matmul(x, w, xs=x_s, **kw)
    r = proj(xr, xr_s, wts["w_r"])
    k = proj(xk, xk_s, wts["w_k"])
    v = proj(xv, xv_s, wts["w_v"])
    tw = proj(xw, xw_s, wts["w1"], act="tanh", out_dtypes=(BF16,))
    ta = proj(xa, xa_s, wts["a1"], out_dtypes=(BF16,))
    sg = proj(xg, xg_s, wts["g1"], act="sigmoid", out_dtypes=(BF16,))
    results = []
    for i, (h, _, s0, nb, t) in enumerate(streams):
        nh = h.shape[1] // RWKV_HEAD_DIM
        decay, k2, kneg, bb, rk = rwkv_prep(r[i], k[i], tw[i], ta[i], wts["w2"], wts["a2"], wts["prm"])
        y, s_t = wkv_scan(_state_to_kernel(s0.astype(F32)), decay, k2, bb, kneg, r[i], v[i], nb, t)
        results.append((rwkv_post(y, v[i], sg[i], wts["g2"], rk, wts["ln"]), _state_from_kernel(s_t, nh)))
    return results


def kernel(x_prompt, x_sample, cache_sb_k, cache_sb_v, page_table, state_rwkv_wkv, state_rwkv_shift, norm_mix, norm_mlp, norm_final, sb_w_qkv, sb_w_o, sb_bias, rw_mu, rw_w_r, rw_w_k, rw_w_v, rw_w_o, rw_w0, rw_w1, rw_w2, rw_a0, rw_a1, rw_a2, rw_g1, rw_g2, rw_k_k, rw_k_a, rw_r_k, rw_ln_w, rw_ln_b, mlp_w1, mlp_w2):
    bp, tp, d = x_prompt.shape
    bs, ts, _ = x_sample.shape
    assert ts == 1, "the paged decode attention handles one new token per sequence"
    depth = norm_mix.shape[0]
    nh_rw = d // RWKV_HEAD_DIM
    n_sb_heads = d // SB_HEAD_DIM

    yp = x_prompt.reshape(bp * tp, d)
    ys = x_sample.reshape(bs * ts, d)
    kp_l, vp_l, ks_l, vs_l = [], [], [], []
    wkvp_l, shp_l, wkvs_l, shs_l = [], [], [], []

    for layer in range(depth):
        li = layer // 2
        if layer % 2 == 0:
            bias = sb_bias[li]
            hp = rmsnorm(yp, norm_mix[layer], BF16)
            hs = rmsnorm(ys, norm_mix[layer], BF16)
            qp, qs = matmul(hp, sb_w_qkv, layer=li, n_off=0, n=d, xs=hs, out_dtypes=(BF16,))
            (kp, kp16), (ks, _) = matmul(hp, sb_w_qkv, layer=li, n_off=d, n=d, xs=hs, out_dtypes=(F32, BF16))
            (vp, vp16), (vs, _) = matmul(hp, sb_w_qkv, layer=li, n_off=2 * d, n=d, xs=hs, out_dtypes=(F32, BF16))
            op = sb_attention_prompt(qp, kp16, vp16, bias, bp, tp)
            os_ = sb_attention_sample(qs, ks, vs, cache_sb_k, cache_sb_v, li, page_table, bias)
            yp, ys = matmul(op, sb_w_o, layer=li, res=yp, xs=os_.astype(BF16), res_s=ys)
            kp_l.append(kp.reshape(bp, tp, n_sb_heads, SB_HEAD_DIM))
            vp_l.append(vp.reshape(bp, tp, n_sb_heads, SB_HEAD_DIM))
            ks_l.append(ks.reshape(bs, ts, n_sb_heads, SB_HEAD_DIM))
            vs_l.append(vs.reshape(bs, ts, n_sb_heads, SB_HEAD_DIM))
        else:
            hm = lambda w: _head_minor_cols(w, nh_rw).astype(BF16)
            g_pad = -rw_g1.shape[-1] % LANES
            wts = {
                "mu": rw_mu[li],
                "w_r": hm(rw_w_r[li]), "w_k": hm(rw_w_k[li]), "w_v": hm(rw_w_v[li]),
                "w1": rw_w1[li].astype(BF16), "w2": hm(rw_w2[li]),
                "a1": rw_a1[li].astype(BF16), "a2": hm(rw_a2[li]),
                "g1": jnp.pad(rw_g1[li], ((0, 0), (0, g_pad))).astype(BF16),
                "g2": hm(jnp.pad(rw_g2[li], ((0, g_pad), (0, 0)))),
                "prm": jnp.concatenate([_head_minor_vec(p, nh_rw) for p in
                                        (rw_w0[li], rw_a0[li], rw_k_k[li], rw_k_a[li], rw_r_k[li])]
                                       + [jnp.zeros((3, d), F32)], axis=0),
                "ln": jnp.concatenate([_head_minor_vec(rw_ln_w[li], nh_rw), _head_minor_vec(rw_ln_b[li], nh_rw)], axis=0),
            }
            w_o = rw_w_o[li].reshape(nh_rw, RWKV_HEAD_DIM, d).swapaxes(0, 1).reshape(d, d).astype(BF16)
            hp = rmsnorm(yp, norm_mix[layer], F32)
            hs = rmsnorm(ys, norm_mix[layer], F32)
            zero_shift = jnp.zeros((bp, d), F32)
            zero_wkv = jnp.zeros((bp, nh_rw, RWKV_HEAD_DIM, RWKV_HEAD_DIM), F32)
            (op, wkvp), (os_, wkvs) = rwkv_time_mix(
                [(hp, zero_shift, zero_wkv, bp, tp), (hs, state_rwkv_shift[li], state_rwkv_wkv[li], bs, ts)], wts)
            yp, ys = matmul(op, w_o, res=yp, xs=os_, res_s=ys)
            wkvp_l.append(wkvp)
            shp_l.append(hp.reshape(bp, tp, d)[:, -1, :])
            wkvs_l.append(wkvs)
            shs_l.append(hs.reshape(bs, ts, d)[:, -1, :])
        w2 = mlp_w2[layer].astype(BF16)
        hid_p, hid_s = matmul(rmsnorm(yp, norm_mlp[layer], BF16), mlp_w1, layer=layer, act="relu2",
                              xs=rmsnorm(ys, norm_mlp[layer], BF16), out_dtypes=(BF16,))
        yp, ys = matmul(hid_p, w2, res=yp, xs=hid_s, res_s=ys, tk=4096)

    y_prompt = rmsnorm(yp, norm_final, F32).reshape(bp, tp, d)
    y_sample = rmsnorm(ys, norm_final, F32).reshape(bs, ts, d)
    return (y_prompt, y_sample, jnp.stack(kp_l), jnp.stack(vp_l), jnp.stack(ks_l), jnp.stack(vs_l),
            jnp.stack(wkvp_l), jnp.stack(shp_l), jnp.stack(wkvs_l), jnp.stack(shs_l))
```

```python
import functools

import jax
import jax.numpy as jnp
from jax import lax
from jax.experimental import pallas as pl
from jax.experimental.pallas import tpu as pltpu

F32 = jnp.float32
BF16 = jnp.bfloat16

LANES = 128
SUBLANES = 8
VMEM_LIMIT_BYTES = 56 * 1024 * 1024

PAGE_SIZE = 128
SB_HEAD_DIM = 128
RWKV_HEAD_DIM = 64
RMS_EPS = 1e-5
GN_EPS = 64e-5


def _params(*sem):
    return pltpu.CompilerParams(dimension_semantics=sem, vmem_limit_bytes=VMEM_LIMIT_BYTES)


def _rmsnorm_kernel(x_ref, g_ref, o_ref):
    x = x_ref[...]
    ms = jnp.mean(x * x, axis=-1, keepdims=True)
    o_ref[...] = ((x * lax.rsqrt(ms + RMS_EPS)) * g_ref[...]).astype(o_ref.dtype)


def rmsnorm(x, g, out_dtype, tm=256):
    m, d = x.shape
    tm = min(tm, m)
    assert m % tm == 0
    return pl.pallas_call(
        _rmsnorm_kernel,
        grid=(m // tm,),
        in_specs=[pl.BlockSpec((tm, d), lambda i: (i, 0)), pl.BlockSpec((1, d), lambda i: (0, 0))],
        out_specs=pl.BlockSpec((tm, d), lambda i: (i, 0)),
        out_shape=jax.ShapeDtypeStruct((m, d), out_dtype),
        compiler_params=_params("parallel"),
        name="rmsnorm",
    )(x, g.reshape(1, d).astype(F32))


def _mm_kernel(*refs, nk, act, has_res, n_out, has_s, w_f32):
    it = iter(refs)
    x_ref, w_ref = next(it), next(it)
    res_ref = next(it) if has_res else None
    xs_ref = next(it) if has_s else None
    ress_ref = next(it) if (has_s and has_res) else None
    out_refs = [next(it) for _ in range(n_out)]
    outs_refs = [next(it) for _ in range(n_out)] if has_s else []
    wb_ref = next(it) if w_f32 else None
    acc_ref = next(it) if nk > 1 else None
    accs_ref = next(it) if (nk > 1 and has_s) else None
    i, k = pl.program_id(1), pl.program_id(2)

    if w_f32:
        @pl.when(i == 0)
        def _():
            wb_ref[...] = w_ref[...].astype(BF16)
        w_ref = wb_ref

    def finish(acc, r_ref, o_refs):
        if act == "relu2":
            r = jnp.maximum(acc, 0.0)
            acc = r * r
        elif act == "tanh":
            acc = jnp.tanh(acc)
        elif act == "sigmoid":
            acc = 1.0 / (1.0 + jnp.exp(-acc))
        if has_res:
            acc = r_ref[...] + acc
        for o in o_refs:
            o[...] = acc.astype(o.dtype)

    def run(lhs_ref, r_ref, o_refs, a_ref):
        part = jnp.dot(lhs_ref[...], w_ref[...], preferred_element_type=F32)
        if nk == 1:
            finish(part, r_ref, o_refs)
            return

        @pl.when(k == 0)
        def _():
            a_ref[...] = part

        @pl.when(k > 0)
        def _():
            a_ref[...] += part

        @pl.when(k == nk - 1)
        def _():
            finish(a_ref[...], r_ref, o_refs)

    run(x_ref, res_ref, out_refs, acc_ref)
    if has_s:
        @pl.when(i == 0)
        def _():
            run(xs_ref, ress_ref, outs_refs, accs_ref)


def matmul(x, w, *, layer=None, n_off=0, n=None, act=None, res=None, xs=None, res_s=None,
           out_dtypes=(F32,), tm=1024, tn=512, tk=None):
    m, kdim = x.shape
    w_f32 = layer is not None
    n = w.shape[-1] if n is None else n
    tm, tn = min(tm, m), min(tn, n)
    tk = kdim if tk is None else min(tk, kdim)
    assert m % tm == 0 and n % tn == 0 and kdim % tk == 0 and n_off % tn == 0 and w.shape[-2] == kdim
    assert not (w_f32 and tk != kdim) and (w_f32 or w.dtype == BF16)
    assert (res_s is None) == (res is None or xs is None)
    nk, joff = kdim // tk, n_off // tn
    has_s, has_res = xs is not None, res is not None
    if w_f32:
        w_spec = pl.BlockSpec((None, tk, tn), lambda j, i, k: (layer, k, j + joff))
    else:
        w_spec = pl.BlockSpec((tk, tn), lambda j, i, k: (k, j + joff))
    in_specs, args = [pl.BlockSpec((tm, tk), lambda j, i, k: (i, k)), w_spec], [x, w]
    if has_res:
        in_specs.append(pl.BlockSpec((tm, tn), lambda j, i, k: (i, j)))
        args.append(res)
    out_specs = [pl.BlockSpec((tm, tn), lambda j, i, k: (i, j)) for _ in out_dtypes]
    out_shape = [jax.ShapeDtypeStruct((m, n), dt) for dt in out_dtypes]
    scratch = [pltpu.VMEM((tk, tn), BF16)] if w_f32 else []
    if nk > 1:
        scratch.append(pltpu.VMEM((tm, tn), F32))
    if has_s:
        ms = xs.shape[0]
        in_specs.append(pl.BlockSpec((ms, tk), lambda j, i, k: (0, k)))
        args.append(xs)
        if has_res:
            in_specs.append(pl.BlockSpec((ms, tn), lambda j, i, k: (0, j)))
            args.append(res_s)
        out_specs += [pl.BlockSpec((ms, tn), lambda j, i, k: (0, j)) for _ in out_dtypes]
        out_shape += [jax.ShapeDtypeStruct((ms, n), dt) for dt in out_dtypes]
        if nk > 1:
            scratch.append(pltpu.VMEM((ms, tn), F32))
    out = pl.pallas_call(
        functools.partial(_mm_kernel, nk=nk, act=act, has_res=has_res, n_out=len(out_dtypes), has_s=has_s,
                          w_f32=w_f32),
        grid=(n // tn, m // tm, nk),
        in_specs=in_specs,
        out_specs=out_specs,
        out_shape=out_shape,
        scratch_shapes=scratch,
        compiler_params=_params("parallel", "arbitrary", "arbitrary"),
        name="matmul_f32w" if w_f32 else "matmul",
    )(*args)
    unpack = lambda o: o[0] if len(out_dtypes) == 1 else tuple(o)
    if has_s:
        return unpack(out[:len(out_dtypes)]), unpack(out[len(out_dtypes):])
    return unpack(out)


def _log_sigmoid_pair(z):
    t = jnp.log(1.0 + jnp.exp(-jnp.abs(z)))
    return jnp.minimum(z, 0.0) - t, jnp.minimum(-z, 0.0) - t


def _suffix_sums(log_rest, u, two_terms=True):
    hi = log_rest.astype(BF16)
    if not two_terms:
        return jnp.dot(hi, u, preferred_element_type=F32)
    lo = (log_rest - hi.astype(F32)).astype(BF16)
    return (jnp.dot(hi, u, preferred_element_type=F32) + jnp.dot(lo, u, preferred_element_type=F32))


def _sb_prompt_kernel(bias_ref, q_ref, k_ref, v_ref, u_ref, o_ref, *, bq, hp, scale):
    hd = SB_HEAD_DIM
    h0 = pl.program_id(1) * hp
    qi = pl.program_id(2)
    u = u_ref[...]
    heads = range(hp)
    qs = [q_ref[:, hh * hd:(hh + 1) * hd] for hh in heads]
    biases = [bias_ref[h0 + hh] for hh in heads]
    n_chunks = bq // LANES

    def group(g, carries, accs, masked):
        start = pl.multiple_of(g * bq, bq)
        zs = []
        for hh in heads:
            k = k_ref[pl.ds(start, bq), hh * hd:(hh + 1) * hd]
            zs.append(lax.dot_general(qs[hh], k, (((1,), (1,)), ((), ())), preferred_element_type=F32)
                      * scale + biases[hh])
        if masked:
            row = lax.broadcasted_iota(jnp.int32, (bq, bq), 0)
            col = lax.broadcasted_iota(jnp.int32, (bq, bq), 1)
            mask = col < row
        ls_pos, ls_neg = [], []
        for hh in heads:
            lp, ln = _log_sigmoid_pair(zs[hh])
            ls_pos.append(lp)
            ls_neg.append(jnp.where(mask, ln, 0.0) if masked else ln)
        cs = [[_suffix_sums(ls_neg[hh][:, c * LANES:(c + 1) * LANES], u, two_terms=False)
               for c in range(n_chunks)] for hh in heads]
        new_carries, new_accs = [], []
        for hh in heads:
            carry = carries[hh]
            between = [None] * n_chunks
            for c in reversed(range(n_chunks)):
                between[c] = cs[hh][c][:, :LANES] + carry
                carry = carry + cs[hh][c][:, LANES:]
            p = jnp.exp(ls_pos[hh] + jnp.concatenate(between, axis=1))
            if masked:
                p = jnp.where(mask, p, 0.0)
            v = v_ref[pl.ds(start, bq), hh * hd:(hh + 1) * hd]
            new_accs.append(accs[hh] + jnp.dot(p.astype(BF16), v, preferred_element_type=F32))
            new_carries.append(carry)
        return new_carries, new_accs

    carries = [jnp.zeros((bq, LANES), F32) for _ in heads]
    accs = [jnp.zeros((bq, hd), F32) for _ in heads]
    carries, accs = group(qi, carries, accs, True)

    def body(i, ca):
        c, a = group(qi - 1 - i, list(ca[0]), list(ca[1]), False)
        return tuple(c), tuple(a)

    carries, accs = lax.fori_loop(0, qi, body, (tuple(carries), tuple(accs)))
    for hh in heads:
        o_ref[:, hh * hd:(hh + 1) * hd] = accs[hh].astype(o_ref.dtype)


def _suffix_matrix():
    j = lax.broadcasted_iota(jnp.int32, (LANES, 2 * LANES), 0)
    s = lax.broadcasted_iota(jnp.int32, (LANES, 2 * LANES), 1)
    return jnp.where((s >= LANES) | (j > s), 1.0, 0.0).astype(BF16)


def sb_attention_prompt(q, k, v, bias, b, t):
    n_heads = q.shape[1] // SB_HEAD_DIM
    hp = 4 if n_heads % 4 == 0 else 1
    bq = 256 if t % 256 == 0 else LANES
    assert t % bq == 0
    nq = t // bq
    wide = hp * SB_HEAD_DIM
    grid_spec = pltpu.PrefetchScalarGridSpec(
        num_scalar_prefetch=1,
        grid=(b, n_heads // hp, nq),
        in_specs=[pl.BlockSpec((bq, wide), lambda bi, h, i, _: (bi * nq + i, h)),
                  pl.BlockSpec((t, wide), lambda bi, h, i, _: (bi, h)),
                  pl.BlockSpec((t, wide), lambda bi, h, i, _: (bi, h)),
                  pl.BlockSpec((LANES, 2 * LANES), lambda bi, h, i, _: (0, 0))],
        out_specs=pl.BlockSpec((bq, wide), lambda bi, h, i, _: (bi * nq + i, h)),
    )
    return pl.pallas_call(
        functools.partial(_sb_prompt_kernel, bq=bq, hp=hp, scale=SB_HEAD_DIM ** -0.5),
        grid_spec=grid_spec,
        out_shape=jax.ShapeDtypeStruct(q.shape, BF16),
        compiler_params=_params("parallel", "parallel", "parallel"),
        name="sb_attention_prompt",
    )(bias.astype(F32), q, k, v, _suffix_matrix())


PAGES_PER_STEP = 2


def _sb_sample_kernel(pt_ref, qbd_ref, bias_ref, u_ref, knew_ref, vnew_ref, kc_hbm, vc_hbm,
                      o_ref, kbuf, vbuf, sem, carry_ref, acc_ref, *, li, n_pages, n_new, scale):
    b = pl.program_id(0)
    s = pl.program_id(1)
    n_seq = pl.num_programs(0)
    n_heads = qbd_ref.shape[0]
    steps = n_pages // PAGES_PER_STEP
    q_pos = n_pages * PAGE_SIZE
    qbd = qbd_ref[...]

    def page_copies(seq, step, slot):
        cps = []
        for j in range(PAGES_PER_STEP):
            page = pt_ref[seq, n_pages - PAGES_PER_STEP * step + (PAGES_PER_STEP - 1 - j)]
            for h in range(n_heads):
                win = pl.ds(h * SB_HEAD_DIM, SB_HEAD_DIM)
                cps.append(pltpu.make_async_copy(kc_hbm.at[li, page, :, h, :], kbuf.at[slot, j, :, win], sem.at[slot]))
                cps.append(pltpu.make_async_copy(vc_hbm.at[li, page, :, h, :], vbuf.at[slot, j, :, win], sem.at[slot]))
        return cps

    @pl.when((b == 0) & (s == 0))
    def _():
        for cp in page_copies(0, 1, 0):
            cp.start()

    @pl.when(s == 0)
    def _():
        carry_ref[...] = jnp.zeros_like(carry_ref)
        acc_ref[...] = jnp.zeros_like(acc_ref)

    def process(pages):
        zs = [lax.dot_general(qbd, k_page.astype(BF16), (((1,), (1,)), ((), ())), preferred_element_type=F32)
              * scale + bias_ref[...] for k_page, _, _, _ in pages]
        idx = lax.broadcasted_iota(jnp.int32, zs[0].shape, 1)
        masks = [((pos0 + idx) < q_pos) & (idx < n_valid) for _, _, pos0, n_valid in pages]
        ls = [_log_sigmoid_pair(z) for z in zs]
        cs = [_suffix_sums(jnp.where(m, ln, 0.0), u_ref[...]) for m, (_, ln) in zip(masks, ls)]
        carry = carry_ref[...]
        atts = []
        for m, (lp, _), c in zip(masks, ls, cs):
            atts.append(jnp.where(m, jnp.exp(lp + (c[:, :PAGE_SIZE] + carry)), 0.0).astype(BF16))
            carry = carry + c[:, PAGE_SIZE:]
        carry_ref[...] = carry
        acc = acc_ref[...]
        for att, (_, v_page, _, _) in zip(atts, pages):
            acc = acc + jnp.dot(att, v_page.astype(BF16), preferred_element_type=F32)
        acc_ref[...] = acc

    @pl.when(s == 0)
    def _():
        first_row = lax.broadcasted_iota(jnp.int32, (PAGE_SIZE, qbd.shape[1]), 0) < n_new
        process([(jnp.where(first_row, knew_ref[...], 0.0), jnp.where(first_row, vnew_ref[...], 0.0), q_pos, n_new)])

    @pl.when(s > 0)
    def _():
        slot = (b * steps + s - 1) % 2

        @pl.when(s < steps)
        def _():
            for cp in page_copies(b, s + 1, 1 - slot):
                cp.start()

        @pl.when((s == steps) & (b + 1 < n_seq))
        def _():
            for cp in page_copies(b + 1, 1, 1 - slot):
                cp.start()

        for cp in page_copies(b, s, slot):
            cp.wait()
        first = (n_pages - PAGES_PER_STEP * s) * PAGE_SIZE
        process([(kbuf[slot, j], vbuf[slot, j], first + (PAGES_PER_STEP - 1 - j) * PAGE_SIZE, PAGE_SIZE)
                 for j in range(PAGES_PER_STEP)])

    @pl.when(s == steps)
    def _():
        acc = acc_ref[...]
        row = lax.broadcasted_iota(jnp.int32, acc.shape, 0)
        col = lax.broadcasted_iota(jnp.int32, acc.shape, 1)
        own = (col // SB_HEAD_DIM) == row
        o_ref[...] = jnp.sum(jnp.where(own, acc, 0.0), axis=0, keepdims=True)


def sb_attention_sample(q, k_new, v_new, cache_k, cache_v, li, page_table, bias):
    bs, d = q.shape
    n_heads = d // SB_HEAD_DIM
    n_pages = page_table.shape[1]
    assert n_pages % PAGES_PER_STEP == 0 and cache_k.shape[2:] == (PAGE_SIZE, n_heads, SB_HEAD_DIM)
    qh = q.reshape(bs, n_heads, 1, SB_HEAD_DIM)
    eye = jnp.eye(n_heads, dtype=F32).reshape(1, n_heads, n_heads, 1)
    qbd = (qh * eye).reshape(bs, n_heads, d).astype(BF16)
    bias_b = jnp.broadcast_to(bias.astype(F32).reshape(n_heads, 1), (n_heads, PAGE_SIZE))
    page_buf = pltpu.VMEM((2, PAGES_PER_STEP, PAGE_SIZE, d), F32)
    grid_spec = pltpu.PrefetchScalarGridSpec(
        num_scalar_prefetch=1,
        grid=(bs, n_pages // PAGES_PER_STEP + 1),
        in_specs=[pl.BlockSpec((None, n_heads, d), lambda b, s, pt: (b, 0, 0)),
                  pl.BlockSpec((n_heads, PAGE_SIZE), lambda b, s, pt: (0, 0)),
                  pl.BlockSpec((LANES, 2 * LANES), lambda b, s, pt: (0, 0)),
                  pl.BlockSpec((None, 1, d), lambda b, s, pt: (b, 0, 0)),
                  pl.BlockSpec((None, 1, d), lambda b, s, pt: (b, 0, 0)),
                  pl.BlockSpec(memory_space=pl.ANY), pl.BlockSpec(memory_space=pl.ANY)],
        out_specs=pl.BlockSpec((None, 1, d), lambda b, s, pt: (b, 0, 0)),
        scratch_shapes=[page_buf, page_buf, pltpu.SemaphoreType.DMA((2,)),
                        pltpu.VMEM((n_heads, PAGE_SIZE), F32), pltpu.VMEM((n_heads, d), F32)],
    )
    out = pl.pallas_call(
        functools.partial(_sb_sample_kernel, li=li, n_pages=n_pages, n_new=1, scale=SB_HEAD_DIM ** -0.5),
        grid_spec=grid_spec,
        out_shape=jax.ShapeDtypeStruct((bs, 1, d), F32),
        compiler_params=_params("arbitrary", "arbitrary"),
        name="sb_attention_sample",
    )(page_table, qbd, bias_b, _suffix_matrix(), k_new.reshape(bs, 1, d), v_new.reshape(bs, 1, d), cache_k, cache_v)
    return out.reshape(bs, d)


def _head_minor_cols(w, n_heads):
    kdim = w.shape[0]
    return w.reshape(kdim, n_heads, RWKV_HEAD_DIM).swapaxes(1, 2).reshape(kdim, n_heads * RWKV_HEAD_DIM)


def _head_minor_vec(p, n_heads):
    return p.reshape(n_heads, RWKV_HEAD_DIM).T.reshape(1, n_heads * RWKV_HEAD_DIM).astype(F32)


def _mix_kernel(h_ref, xp_ref, mu_ref, *o_refs):
    h = h_ref[...]
    xx = xp_ref[...] - h
    for i, o in enumerate(o_refs):
        o[...] = (h + xx * mu_ref[i:i + 1, :]).astype(o.dtype)


def token_shift_mix(h, x_prev, mu, tm=128):
    m, d = h.shape
    tm = min(tm, m)
    assert m % tm == 0
    n_mix = mu.shape[0]
    row = pl.BlockSpec((tm, d), lambda i: (i, 0))
    return pl.pallas_call(
        _mix_kernel,
        grid=(m // tm,),
        in_specs=[row, row, pl.BlockSpec((n_mix, d), lambda i: (0, 0))],
        out_specs=[row] * n_mix,
        out_shape=[jax.ShapeDtypeStruct((m, d), BF16)] * n_mix,
        compiler_params=_params("parallel"),
        name="token_shift_mix",
    )(h, x_prev, mu.astype(F32))


def _norm_shift_mix_kernel(y_ref, yprev_ref, g_ref, shift_ref, mu_ref, *refs, t, tm):
    o_refs, hlast_ref = refs[:-1], refs[-1]
    i = pl.program_id(0)

    def norm(y):
        return (y * lax.rsqrt(jnp.mean(y * y, axis=-1, keepdims=True) + RMS_EPS)) * g_ref[...]

    h = norm(y_ref[...])
    above = norm(yprev_ref[SUBLANES - 1:SUBLANES, :])
    first = jnp.where((i * tm) % t == 0, shift_ref[...], above)
    row = lax.broadcasted_iota(jnp.int32, h.shape, 0)
    x_prev = jnp.where(row == 0, first, pltpu.roll(h, 1, 0))
    xx = x_prev - h
    for j, o in enumerate(o_refs):
        o[...] = (h + xx * mu_ref[j:j + 1, :]).astype(o.dtype)

    @pl.when(((i + 1) * tm) % t == 0)
    def _():
        hlast_ref[...] = h[tm - 1:tm, :]


def norm_shift_mix(y, g, shift_prev, mu, nb, t, tm=128):
    m, d = y.shape
    tm = min(tm, t)
    assert t % tm == 0 and tm % SUBLANES == 0 and m == nb * t
    n_mix = mu.shape[0]
    row = pl.BlockSpec((tm, d), lambda i: (i, 0))
    per_seq = pl.BlockSpec((None, 1, d), lambda i: ((i * tm) // t, 0, 0))
    outs = pl.pallas_call(
        functools.partial(_norm_shift_mix_kernel, t=t, tm=tm),
        grid=(m // tm,),
        in_specs=[row,
                  pl.BlockSpec((SUBLANES, d), lambda i: (jnp.maximum(i * (tm // SUBLANES) - 1, 0), 0)),
                  pl.BlockSpec((1, d), lambda i: (0, 0)),
                  per_seq,
                  pl.BlockSpec((n_mix, d), lambda i: (0, 0))],
        out_specs=[row] * n_mix + [per_seq],
        out_shape=[jax.ShapeDtypeStruct((m, d), BF16)] * n_mix + [jax.ShapeDtypeStruct((nb, 1, d), F32)],
        compiler_params=_params("arbitrary"),
        name="norm_shift_mix",
    )(y, y, g.reshape(1, d).astype(F32), shift_prev.reshape(nb, 1, d).astype(F32), mu.astype(F32))
    return outs[:n_mix], outs[n_mix].reshape(nb, d)


def _head_sum(x):
    s = x[:, 0:LANES]
    for m in range(1, x.shape[1] // LANES):
        s = s + x[:, m * LANES:(m + 1) * LANES]
    return s + pltpu.roll(s, LANES // 2, 1)


def _tile_lanes(s, d):
    return jnp.concatenate([s] * (d // LANES), axis=1)


def _rwkv_prep_kernel(r_ref, k_ref, tw_ref, ta_ref, w2_ref, a2_ref, prm_ref,
                      decay_ref, k2_ref, kneg_ref, bb_ref, rk_ref):
    d = r_ref.shape[1]
    w0, a0, k_k, k_a, r_k = (prm_ref[i:i + 1, :] for i in range(5))
    u = -(w0 + jnp.dot(tw_ref[...], w2_ref[...], preferred_element_type=F32))
    softplus = jnp.maximum(u, 0.0) + jnp.log(1.0 + jnp.exp(-jnp.abs(u)))
    decay_ref[...] = jnp.exp(-jnp.exp(-softplus - 0.5))
    a = 1.0 / (1.0 + jnp.exp(-(a0 + jnp.dot(ta_ref[...], a2_ref[...], preferred_element_type=F32))))
    k = k_ref[...]
    kk = k * k_k
    kk = kk * _tile_lanes(jnp.minimum(lax.rsqrt(_head_sum(kk * kk)), 1e12), d)
    k2 = k * (1.0 + (a - 1.0) * k_a)
    k2_ref[...] = k2
    kneg_ref[...] = -kk
    bb_ref[...] = kk * a
    rk_ref[...] = _head_sum(r_ref[...] * k2 * r_k)


def rwkv_prep(r, k, tw, ta, w2, a2, prm, tm=64):
    m, d = r.shape
    tm = min(tm, m)
    assert m % tm == 0
    row = pl.BlockSpec((tm, d), lambda i: (i, 0))
    lora = lambda x: pl.BlockSpec((tm, x.shape[1]), lambda i: (i, 0))
    whole = lambda x: pl.BlockSpec(x.shape, lambda i: (0, 0))
    full = jax.ShapeDtypeStruct((m, d), F32)
    return pl.pallas_call(
        _rwkv_prep_kernel,
        grid=(m // tm,),
        in_specs=[row, row, lora(tw), lora(ta), whole(w2), whole(a2), whole(prm)],
        out_specs=[row, row, row, row, pl.BlockSpec((tm, LANES), lambda i: (i, 0))],
        out_shape=[full, full, full, full, jax.ShapeDtypeStruct((m, LANES), F32)],
        compiler_params=_params("parallel"),
        name="rwkv_prep",
    )(r, k, tw, ta, w2, a2, prm)


N_PARTIAL_SUMS = 2


def _wkv_kernel(s0_ref, w_ref, k_ref, b_ref, a_ref, r_ref, v_ref, y_ref, st_ref, s_ref, *, tc, ns, nq, nm):
    c = pl.program_id(1)

    @pl.when(c == 0)
    def _():
        s_ref[...] = s0_ref[...]

    lane = lax.broadcasted_iota(jnp.int32, (SUBLANES, LANES), 1)
    lo_half = lane < (LANES // 2)
    swap = lambda x: pltpu.roll(x, LANES // 2, 1)
    rows = [(q, par) for q in range(nq) for par in range(2)]
    seqs = range(ns)

    def row(ref, e, t, m):
        return jnp.broadcast_to(ref[e, t, m:m + 1, :], (SUBLANES, LANES))

    def sum_partials(acc, key):
        tot = acc[key + (0,)]
        for i in range(1, min(N_PARTIAL_SUMS, nm)):
            tot = tot + acc[key + (i,)]
        return tot

    def step(t, _):
        sa_part = {}
        for e in seqs:
            for m in range(nm):
                a_m = row(a_ref, e, t, m)
                for qp in rows:
                    pr = s_ref[e, qp[0], qp[1], m] * a_m
                    key = (e, qp, m % N_PARTIAL_SUMS)
                    sa_part[key] = pr if key not in sa_part else sa_part[key] + pr
        sa, vt = {}, {}
        for e in seqs:
            for qp in rows:
                tot = sum_partials(sa_part, (e, qp))
                sa[(e, qp)] = tot + swap(tot)
            for q in range(nq):
                x = v_ref[e, t, q * SUBLANES:(q + 1) * SUBLANES, :]
                xs = swap(x)
                vt[(e, (q, 0))] = jnp.where(lo_half, x, xs)
                vt[(e, (q, 1))] = jnp.where(lo_half, xs, x)
        ya = {}
        for e in seqs:
            for m in range(nm):
                w_m, b_m, k_m, r_m = (row(ref, e, t, m) for ref in (w_ref, b_ref, k_ref, r_ref))
                for qp in rows:
                    s_new = s_ref[e, qp[0], qp[1], m] * w_m + sa[(e, qp)] * b_m + vt[(e, qp)] * k_m
                    s_ref[e, qp[0], qp[1], m] = s_new
                    pr = s_new * r_m
                    key = (e, qp, m % N_PARTIAL_SUMS)
                    ya[key] = pr if key not in ya else ya[key] + pr
        for e in seqs:
            for q in range(nq):
                ys = []
                for par in range(2):
                    tot = sum_partials(ya, (e, (q, par)))
                    ys.append(tot + swap(tot))
                y_ref[e, t, q * SUBLANES:(q + 1) * SUBLANES, :] = jnp.where(lo_half, ys[0], ys[1])
        return 0

    lax.fori_loop(0, tc, step, 0)

    @pl.when(c == pl.num_programs(1) - 1)
    def _():
        st_ref[...] = s_ref[...]


def wkv_scan(s0, w, k, b, a, r, v, nb, t):
    d = w.shape[1]
    nq, nm = s0.shape[1], s0.shape[3]
    ns = 2 if nb % 2 == 0 else 1
    tc = 32 if t % 32 == 0 else t
    assert t % tc == 0 and (tc % SUBLANES == 0 or tc == t)
    seq3 = lambda x: x.reshape(nb, t, d // LANES, LANES)
    col3 = pl.BlockSpec((ns, tc, d // LANES, LANES), lambda bi, c: (bi, c, 0, 0))
    st = pl.BlockSpec((ns,) + s0.shape[1:], lambda bi, c: (bi, 0, 0, 0, 0, 0))
    y, s_t = pl.pallas_call(
        functools.partial(_wkv_kernel, tc=tc, ns=ns, nq=nq, nm=nm),
        grid=(nb // ns, t // tc),
        in_specs=[st, col3, col3, col3, col3, col3, col3],
        out_specs=[col3, st],
        out_shape=[jax.ShapeDtypeStruct((nb, t, d // LANES, LANES), F32), jax.ShapeDtypeStruct(s0.shape, F32)],
        scratch_shapes=[pltpu.VMEM((ns,) + s0.shape[1:], F32)],
        compiler_params=_params("parallel", "arbitrary"),
        name="wkv_scan",
    )(s0, seq3(w), seq3(k), seq3(b), seq3(a), seq3(r), seq3(v))
    return y.reshape(nb * t, d), s_t


def _state_to_kernel(s):
    nb, nh, n, _ = s.shape
    x = s.reshape(nb, nh, n // 16, SUBLANES, 2, n // 2, 2)
    x = x.transpose(0, 2, 4, 5, 3, 6, 1)
    return x.reshape(nb, n // 16, 2, n // 2, SUBLANES, 2 * nh)


def _state_from_kernel(x, nh):
    nb, nq, _, nm, _, _ = x.shape
    x = x.reshape(nb, nq, 2, nm, SUBLANES, 2, nh)
    x = x.transpose(0, 6, 1, 4, 2, 3, 5)
    return x.reshape(nb, nh, nq * 16, nm * 2)


def _rwkv_post_kernel(y_ref, v_ref, sg_ref, g2_ref, rk_ref, ln_ref, o_ref):
    d = y_ref.shape[1]
    gate = jnp.dot(sg_ref[...], g2_ref[...], preferred_element_type=F32)
    n = RWKV_HEAD_DIM
    y = y_ref[...]
    mean = _tile_lanes(_head_sum(y), d) * (1.0 / n)
    yc = y - mean
    var = _tile_lanes(_head_sum(yc * yc), d) * (1.0 / n)
    yn = (yc * lax.rsqrt(var + GN_EPS)) * ln_ref[0:1, :] + ln_ref[1:2, :]
    bonus = _tile_lanes(rk_ref[...], d) * v_ref[...]
    o_ref[...] = ((yn + bonus) * gate).astype(o_ref.dtype)


def rwkv_post(y, v, sg, g2, rk, ln, tm=128):
    m, d = y.shape
    tm = min(tm, m)
    assert m % tm == 0
    row = pl.BlockSpec((tm, d), lambda i: (i, 0))
    return pl.pallas_call(
        _rwkv_post_kernel,
        grid=(m // tm,),
        in_specs=[row, row, pl.BlockSpec((tm, sg.shape[1]), lambda i: (i, 0)), pl.BlockSpec(g2.shape, lambda i: (0, 0)),
                  pl.BlockSpec((tm, LANES), lambda i: (i, 0)), pl.BlockSpec(ln.shape, lambda i: (0, 0))],
        out_specs=row,
        out_shape=jax.ShapeDtypeStruct((m, d), BF16),
        compiler_params=_params("parallel"),
        name="rwkv_post",
    )(y, v, sg, g2, rk, ln)


def normed_shift_mixes(y, g, shift_prev, mu, nb, t):
    d = y.shape[1]
    if t % LANES == 0:
        return norm_shift_mix(y, g, shift_prev, mu, nb, t)
    h = rmsnorm(y, g, F32)
    h3 = h.reshape(nb, t, d)
    x_prev = jnp.concatenate([shift_prev[:, None, :].astype(F32), h3[:, :-1]], axis=1).reshape(nb * t, d)
    return token_shift_mix(h, x_prev, mu), h3[:, -1, :]


def rwkv_time_mix(streams, wts):
    (xr, xw, xk, xv, xa, xg), (xr_s, xw_s, xk_s, xv_s, xa_s, xg_s) = [s[0] for s in streams]
    proj = lambda x, x_s, w, **kw: matmul(x, w, xs=x_s, **kw)
    r = proj(xr, xr_s, wts["w_r"])
    k = proj(xk, xk_s, wts["w_k"])
    v = proj(xv, xv_s, wts["w_v"])
    tw = proj(xw, xw_s, wts["w1"], act="tanh", out_dtypes=(BF16,))
    ta = proj(xa, xa_s, wts["a1"], out_dtypes=(BF16,))
    sg = proj(xg, xg_s, wts["g1"], act="sigmoid", out_dtypes=(BF16,))
    results = []
    for i, (_, s0, nb, t) in enumerate(streams):
        nh = s0.shape[1]
        decay, k2, kneg, bb, rk = rwkv_prep(r[i], k[i], tw[i], ta[i], wts["w2"], wts["a2"], wts["prm"])
        y, s_t = wkv_scan(_state_to_kernel(s0.astype(F32)), decay, k2, bb, kneg, r[i], v[i], nb, t)
        results.append((rwkv_post(y, v[i], sg[i], wts["g2"], rk, wts["ln"]), _state_from_kernel(s_t, nh)))
    return results


def kernel(x_prompt, x_sample, cache_sb_k, cache_sb_v, page_table, state_rwkv_wkv, state_rwkv_shift, norm_mix, norm_mlp, norm_final, sb_w_qkv, sb_w_o, sb_bias, rw_mu, rw_w_r, rw_w_k, rw_w_v, rw_w_o, rw_w0, rw_w1, rw_w2, rw_a0, rw_a1, rw_a2, rw_g1, rw_g2, rw_k_k, rw_k_a, rw_r_k, rw_ln_w, rw_ln_b, mlp_w1, mlp_w2):
    bp, tp, d = x_prompt.shape
    bs, ts, _ = x_sample.shape
    assert ts == 1, "the paged decode attention handles one new token per sequence"
    depth = norm_mix.shape[0]
    nh_rw = d // RWKV_HEAD_DIM
    n_sb_heads = d // SB_HEAD_DIM

    yp = x_prompt.reshape(bp * tp, d)
    ys = x_sample.reshape(bs * ts, d)
    kp_l, vp_l, ks_l, vs_l = [], [], [], []
    wkvp_l, shp_l, wkvs_l, shs_l = [], [], [], []

    for layer in range(depth):
        li = layer // 2
        if layer % 2 == 0:
            bias = sb_bias[li]
            hp = rmsnorm(yp, norm_mix[layer], BF16)
            hs = rmsnorm(ys, norm_mix[layer], BF16)
            qp, qs = matmul(hp, sb_w_qkv, layer=li, n_off=0, n=d, xs=hs, out_dtypes=(BF16,))
            (kp, kp16), (ks, _) = matmul(hp, sb_w_qkv, layer=li, n_off=d, n=d, xs=hs, out_dtypes=(F32, BF16))
            (vp, vp16), (vs, _) = matmul(hp, sb_w_qkv, layer=li, n_off=2 * d, n=d, xs=hs, out_dtypes=(F32, BF16))
            op = sb_attention_prompt(qp, kp16, vp16, bias, bp, tp)
            os_ = sb_attention_sample(qs, ks, vs, cache_sb_k, cache_sb_v, li, page_table, bias)
            yp, ys = matmul(op, sb_w_o, layer=li, res=yp, xs=os_.astype(BF16), res_s=ys)
            kp_l.append(kp.reshape(bp, tp, n_sb_heads, SB_HEAD_DIM))
            vp_l.append(vp.reshape(bp, tp, n_sb_heads, SB_HEAD_DIM))
            ks_l.append(ks.reshape(bs, ts, n_sb_heads, SB_HEAD_DIM))
            vs_l.append(vs.reshape(bs, ts, n_sb_heads, SB_HEAD_DIM))
        else:
            hm = lambda w: _head_minor_cols(w, nh_rw).astype(BF16)
            g_pad = -rw_g1.shape[-1] % LANES
            wts = {
                "w_r": hm(rw_w_r[li]), "w_k": hm(rw_w_k[li]), "w_v": hm(rw_w_v[li]),
                "w1": rw_w1[li].astype(BF16), "w2": hm(rw_w2[li]),
                "a1": rw_a1[li].astype(BF16), "a2": hm(rw_a2[li]),
                "g1": jnp.pad(rw_g1[li], ((0, 0), (0, g_pad))).astype(BF16),
                "g2": hm(jnp.pad(rw_g2[li], ((0, g_pad), (0, 0)))),
                "prm": jnp.concatenate([_head_minor_vec(p, nh_rw) for p in
                                        (rw_w0[li], rw_a0[li], rw_k_k[li], rw_k_a[li], rw_r_k[li])]
                                       + [jnp.zeros((3, d), F32)], axis=0),
                "ln": jnp.concatenate([_head_minor_vec(rw_ln_w[li], nh_rw), _head_minor_vec(rw_ln_b[li], nh_rw)], axis=0),
            }
            w_o = rw_w_o[li].reshape(nh_rw, RWKV_HEAD_DIM, d).swapaxes(0, 1).reshape(d, d).astype(BF16)
            zero_shift = jnp.zeros((bp, d), F32)
            zero_wkv = jnp.zeros((bp, nh_rw, RWKV_HEAD_DIM, RWKV_HEAD_DIM), F32)
            mix_p, shp = normed_shift_mixes(yp, norm_mix[layer], zero_shift, rw_mu[li], bp, tp)
            mix_s, shs = normed_shift_mixes(ys, norm_mix[layer], state_rwkv_shift[li], rw_mu[li], bs, ts)
            (op, wkvp), (os_, wkvs) = rwkv_time_mix(
                [(mix_p, zero_wkv, bp, tp), (mix_s, state_rwkv_wkv[li], bs, ts)], wts)
            yp, ys = matmul(op, w_o, res=yp, xs=os_, res_s=ys)
            wkvp_l.append(wkvp)
            shp_l.append(shp)
            wkvs_l.append(wkvs)
            shs_l.append(shs)
        w2 = mlp_w2[layer].astype(BF16)
        hid_p, hid_s = matmul(rmsnorm(yp, norm_mlp[layer], BF16), mlp_w1, layer=layer, act="relu2",
                              xs=rmsnorm(ys, norm_mlp[layer], BF16), out_dtypes=(BF16,))
        yp, ys = matmul(hid_p, w2, res=yp, xs=hid_s, res_s=ys, tk=4096)

    y_prompt = rmsnorm(yp, norm_final, F32).reshape(bp, tp, d)
    y_sample = rmsnorm(ys, norm_final, F32).reshape(bs, ts, d)
    return (y_prompt, y_sample, jnp.stack(kp_l), jnp.stack(vp_l), jnp.stack(ks_l), jnp.stack(vs_l),
            jnp.stack(wkvp_l), jnp.stack(shp_l), jnp.stack(wkvs_l), jnp.stack(shs_l))
```

```python
import functools

import jax
import jax.numpy as jnp
from jax import lax
from jax.experimental import pallas as pl
from jax.experimental.pallas import tpu as pltpu

F32 = jnp.float32
BF16 = jnp.bfloat16

LANES = 128
SUBLANES = 8
VMEM_LIMIT_BYTES = 56 * 1024 * 1024

PAGE_SIZE = 128
SB_HEAD_DIM = 128
RWKV_HEAD_DIM = 64
RMS_EPS = 1e-5
GN_EPS = 64e-5


def _params(*sem):
    return pltpu.CompilerParams(dimension_semantics=sem, vmem_limit_bytes=VMEM_LIMIT_BYTES)


def _rmsnorm_kernel(x_ref, g_ref, o_ref):
    x = x_ref[...]
    ms = jnp.mean(x * x, axis=-1, keepdims=True)
    o_ref[...] = ((x * lax.rsqrt(ms + RMS_EPS)) * g_ref[...]).astype(o_ref.dtype)


def rmsnorm(x, g, out_dtype, tm=256):
    m, d = x.shape
    tm = min(tm, m)
    assert m % tm == 0
    return pl.pallas_call(
        _rmsnorm_kernel,
        grid=(m // tm,),
        in_specs=[pl.BlockSpec((tm, d), lambda i: (i, 0)), pl.BlockSpec((1, d), lambda i: (0, 0))],
        out_specs=pl.BlockSpec((tm, d), lambda i: (i, 0)),
        out_shape=jax.ShapeDtypeStruct((m, d), out_dtype),
        compiler_params=_params("parallel"),
        name="rmsnorm",
    )(x, g.reshape(1, d).astype(F32))


def _mm_kernel(*refs, nk, act, has_res, n_out, has_s, w_f32):
    it = iter(refs)
    x_ref, w_ref = next(it), next(it)
    res_ref = next(it) if has_res else None
    xs_ref = next(it) if has_s else None
    ress_ref = next(it) if (has_s and has_res) else None
    out_refs = [next(it) for _ in range(n_out)]
    outs_refs = [next(it) for _ in range(n_out)] if has_s else []
    wb_ref = next(it) if w_f32 else None
    acc_ref = next(it) if nk > 1 else None
    accs_ref = next(it) if (nk > 1 and has_s) else None
    i, k = pl.program_id(1), pl.program_id(2)

    if w_f32:
        @pl.when(i == 0)
        def _():
            wb_ref[...] = w_ref[...].astype(BF16)
        w_ref = wb_ref

    def finish(acc, r_ref, o_refs):
        if act == "relu2":
            r = jnp.maximum(acc, 0.0)
            acc = r * r
        elif act == "tanh":
            acc = jnp.tanh(acc)
        elif act == "sigmoid":
            acc = 1.0 / (1.0 + jnp.exp(-acc))
        if has_res:
            acc = r_ref[...] + acc
        for o in o_refs:
            o[...] = acc.astype(o.dtype)

    def run(lhs_ref, r_ref, o_refs, a_ref):
        part = jnp.dot(lhs_ref[...], w_ref[...], preferred_element_type=F32)
        if nk == 1:
            finish(part, r_ref, o_refs)
            return

        @pl.when(k == 0)
        def _():
            a_ref[...] = part

        @pl.when(k > 0)
        def _():
            a_ref[...] += part

        @pl.when(k == nk - 1)
        def _():
            finish(a_ref[...], r_ref, o_refs)

    run(x_ref, res_ref, out_refs, acc_ref)
    if has_s:
        @pl.when(i == 0)
        def _():
            run(xs_ref, ress_ref, outs_refs, accs_ref)


def matmul(x, w, *, layer=None, n_off=0, n=None, act=None, res=None, xs=None, res_s=None,
           out_dtypes=(F32,), tm=1024, tn=512, tk=None):
    m, kdim = x.shape
    w_f32 = w.dtype == F32
    assert (layer is not None) == (w.ndim == 3)
    n = w.shape[-1] if n is None else n
    tm, tn = min(tm, m), min(tn, n)
    tk = kdim if tk is None else min(tk, kdim)
    assert m % tm == 0 and n % tn == 0 and kdim % tk == 0 and n_off % tn == 0 and w.shape[-2] == kdim
    assert not (w_f32 and tk != kdim) and (w_f32 or w.dtype == BF16)
    assert (res_s is None) == (res is None or xs is None)
    nk, joff = kdim // tk, n_off // tn
    has_s, has_res = xs is not None, res is not None
    if layer is not None:
        w_spec = pl.BlockSpec((None, tk, tn), lambda j, i, k: (layer, k, j + joff))
    else:
        w_spec = pl.BlockSpec((tk, tn), lambda j, i, k: (k, j + joff))
    in_specs, args = [pl.BlockSpec((tm, tk), lambda j, i, k: (i, k)), w_spec], [x, w]
    if has_res:
        in_specs.append(pl.BlockSpec((tm, tn), lambda j, i, k: (i, j)))
        args.append(res)
    out_specs = [pl.BlockSpec((tm, tn), lambda j, i, k: (i, j)) for _ in out_dtypes]
    out_shape = [jax.ShapeDtypeStruct((m, n), dt) for dt in out_dtypes]
    scratch = [pltpu.VMEM((tk, tn), BF16)] if w_f32 else []
    if nk > 1:
        scratch.append(pltpu.VMEM((tm, tn), F32))
    if has_s:
        ms = xs.shape[0]
        in_specs.append(pl.BlockSpec((ms, tk), lambda j, i, k: (0, k)))
        args.append(xs)
        if has_res:
            in_specs.append(pl.BlockSpec((ms, tn), lambda j, i, k: (0, j)))
            args.append(res_s)
        out_specs += [pl.BlockSpec((ms, tn), lambda j, i, k: (0, j)) for _ in out_dtypes]
        out_shape += [jax.ShapeDtypeStruct((ms, n), dt) for dt in out_dtypes]
        if nk > 1:
            scratch.append(pltpu.VMEM((ms, tn), F32))
    out = pl.pallas_call(
        functools.partial(_mm_kernel, nk=nk, act=act, has_res=has_res, n_out=len(out_dtypes), has_s=has_s,
                          w_f32=w_f32),
        grid=(n // tn, m // tm, nk),
        in_specs=in_specs,
        out_specs=out_specs,
        out_shape=out_shape,
        scratch_shapes=scratch,
        compiler_params=_params("parallel", "arbitrary", "arbitrary"),
        name="matmul_f32w" if w_f32 else "matmul",
    )(*args)
    unpack = lambda o: o[0] if len(out_dtypes) == 1 else tuple(o)
    if has_s:
        return unpack(out[:len(out_dtypes)]), unpack(out[len(out_dtypes):])
    return unpack(out)


def _log_sigmoid_pair(z):
    t = jnp.log(1.0 + jnp.exp(-jnp.abs(z)))
    return jnp.minimum(z, 0.0) - t, jnp.minimum(-z, 0.0) - t


def _suffix_sums(log_rest, u, two_terms=True):
    hi = log_rest.astype(BF16)
    if not two_terms:
        return jnp.dot(hi, u, preferred_element_type=F32)
    lo = (log_rest - hi.astype(F32)).astype(BF16)
    return (jnp.dot(hi, u, preferred_element_type=F32) + jnp.dot(lo, u, preferred_element_type=F32))


def _sb_prompt_kernel(bias_ref, q_ref, k_ref, v_ref, u_ref, o_ref, *, bq, hp, scale):
    hd = SB_HEAD_DIM
    h0 = pl.program_id(1) * hp
    qi = pl.program_id(2)
    u = u_ref[...]
    heads = range(hp)
    qs = [q_ref[:, hh * hd:(hh + 1) * hd] for hh in heads]
    biases = [bias_ref[h0 + hh] for hh in heads]
    n_chunks = bq // LANES

    def group(g, carries, accs, masked):
        start = pl.multiple_of(g * bq, bq)
        zs = []
        for hh in heads:
            k = k_ref[pl.ds(start, bq), hh * hd:(hh + 1) * hd]
            zs.append(lax.dot_general(qs[hh], k, (((1,), (1,)), ((), ())), preferred_element_type=F32)
                      * scale + biases[hh])
        if masked:
            row = lax.broadcasted_iota(jnp.int32, (bq, bq), 0)
            col = lax.broadcasted_iota(jnp.int32, (bq, bq), 1)
            mask = col < row
        ls_pos, ls_neg = [], []
        for hh in heads:
            lp, ln = _log_sigmoid_pair(zs[hh])
            ls_pos.append(lp)
            ls_neg.append(jnp.where(mask, ln, 0.0) if masked else ln)
        cs = [[_suffix_sums(ls_neg[hh][:, c * LANES:(c + 1) * LANES], u, two_terms=False)
               for c in range(n_chunks)] for hh in heads]
        new_carries, new_accs = [], []
        for hh in heads:
            carry = carries[hh]
            between = [None] * n_chunks
            for c in reversed(range(n_chunks)):
                between[c] = cs[hh][c][:, :LANES] + carry
                carry = carry + cs[hh][c][:, LANES:]
            p = jnp.exp(ls_pos[hh] + jnp.concatenate(between, axis=1))
            if masked:
                p = jnp.where(mask, p, 0.0)
            v = v_ref[pl.ds(start, bq), hh * hd:(hh + 1) * hd]
            new_accs.append(accs[hh] + jnp.dot(p.astype(BF16), v, preferred_element_type=F32))
            new_carries.append(carry)
        return new_carries, new_accs

    carries = [jnp.zeros((bq, LANES), F32) for _ in heads]
    accs = [jnp.zeros((bq, hd), F32) for _ in heads]
    carries, accs = group(qi, carries, accs, True)

    def body(i, ca):
        c, a = group(qi - 1 - i, list(ca[0]), list(ca[1]), False)
        return tuple(c), tuple(a)

    carries, accs = lax.fori_loop(0, qi, body, (tuple(carries), tuple(accs)))
    for hh in heads:
        o_ref[:, hh * hd:(hh + 1) * hd] = accs[hh].astype(o_ref.dtype)


def _suffix_matrix():
    j = lax.broadcasted_iota(jnp.int32, (LANES, 2 * LANES), 0)
    s = lax.broadcasted_iota(jnp.int32, (LANES, 2 * LANES), 1)
    return jnp.where((s >= LANES) | (j > s), 1.0, 0.0).astype(BF16)


def sb_attention_prompt(q, k, v, bias, b, t):
    n_heads = q.shape[1] // SB_HEAD_DIM
    hp = 4 if n_heads % 4 == 0 else 1
    bq = 256 if t % 256 == 0 else LANES
    assert t % bq == 0
    nq = t // bq
    wide = hp * SB_HEAD_DIM
    grid_spec = pltpu.PrefetchScalarGridSpec(
        num_scalar_prefetch=1,
        grid=(b, n_heads // hp, nq),
        in_specs=[pl.BlockSpec((bq, wide), lambda bi, h, i, _: (bi * nq + i, h)),
                  pl.BlockSpec((t, wide), lambda bi, h, i, _: (bi, h)),
                  pl.BlockSpec((t, wide), lambda bi, h, i, _: (bi, h)),
                  pl.BlockSpec((LANES, 2 * LANES), lambda bi, h, i, _: (0, 0))],
        out_specs=pl.BlockSpec((bq, wide), lambda bi, h, i, _: (bi * nq + i, h)),
    )
    return pl.pallas_call(
        functools.partial(_sb_prompt_kernel, bq=bq, hp=hp, scale=SB_HEAD_DIM ** -0.5),
        grid_spec=grid_spec,
        out_shape=jax.ShapeDtypeStruct(q.shape, BF16),
        compiler_params=_params("parallel", "parallel", "parallel"),
        name="sb_attention_prompt",
    )(bias.astype(F32), q, k, v, _suffix_matrix())


PAGES_PER_STEP = 2


def _sb_sample_kernel(pt_ref, qbd_ref, bias_ref, u_ref, knew_ref, vnew_ref, kc_hbm, vc_hbm,
                      o_ref, kbuf, vbuf, sem, carry_ref, acc_ref, *, li, n_pages, n_new, scale):
    b = pl.program_id(0)
    s = pl.program_id(1)
    n_seq = pl.num_programs(0)
    n_heads = qbd_ref.shape[0]
    steps = n_pages // PAGES_PER_STEP
    q_pos = n_pages * PAGE_SIZE
    qbd = qbd_ref[...]

    def page_copies(seq, step, slot):
        cps = []
        for j in range(PAGES_PER_STEP):
            page = pt_ref[seq, n_pages - PAGES_PER_STEP * step + (PAGES_PER_STEP - 1 - j)]
            for h in range(n_heads):
                win = pl.ds(h * SB_HEAD_DIM, SB_HEAD_DIM)
                cps.append(pltpu.make_async_copy(kc_hbm.at[li, page, :, h, :], kbuf.at[slot, j, :, win], sem.at[slot]))
                cps.append(pltpu.make_async_copy(vc_hbm.at[li, page, :, h, :], vbuf.at[slot, j, :, win], sem.at[slot]))
        return cps

    @pl.when((b == 0) & (s == 0))
    def _():
        for cp in page_copies(0, 1, 0):
            cp.start()

    @pl.when(s == 0)
    def _():
        carry_ref[...] = jnp.zeros_like(carry_ref)
        acc_ref[...] = jnp.zeros_like(acc_ref)

    def process(pages):
        zs = [lax.dot_general(qbd, k_page.astype(BF16), (((1,), (1,)), ((), ())), preferred_element_type=F32)
              * scale + bias_ref[...] for k_page, _, _, _ in pages]
        idx = lax.broadcasted_iota(jnp.int32, zs[0].shape, 1)
        masks = [((pos0 + idx) < q_pos) & (idx < n_valid) for _, _, pos0, n_valid in pages]
        ls = [_log_sigmoid_pair(z) for z in zs]
        cs = [_suffix_sums(jnp.where(m, ln, 0.0), u_ref[...]) for m, (_, ln) in zip(masks, ls)]
        carry = carry_ref[...]
        atts = []
        for m, (lp, _), c in zip(masks, ls, cs):
            atts.append(jnp.where(m, jnp.exp(lp + (c[:, :PAGE_SIZE] + carry)), 0.0).astype(BF16))
            carry = carry + c[:, PAGE_SIZE:]
        carry_ref[...] = carry
        acc = acc_ref[...]
        for att, (_, v_page, _, _) in zip(atts, pages):
            acc = acc + jnp.dot(att, v_page.astype(BF16), preferred_element_type=F32)
        acc_ref[...] = acc

    @pl.when(s == 0)
    def _():
        first_row = lax.broadcasted_iota(jnp.int32, (PAGE_SIZE, qbd.shape[1]), 0) < n_new
        process([(jnp.where(first_row, knew_ref[...], 0.0), jnp.where(first_row, vnew_ref[...], 0.0), q_pos, n_new)])

    @pl.when(s > 0)
    def _():
        slot = (b * steps + s - 1) % 2

        @pl.when(s < steps)
        def _():
            for cp in page_copies(b, s + 1, 1 - slot):
                cp.start()

        @pl.when((s == steps) & (b + 1 < n_seq))
        def _():
            for cp in page_copies(b + 1, 1, 1 - slot):
                cp.start()

        for cp in page_copies(b, s, slot):
            cp.wait()
        first = (n_pages - PAGES_PER_STEP * s) * PAGE_SIZE
        process([(kbuf[slot, j], vbuf[slot, j], first + (PAGES_PER_STEP - 1 - j) * PAGE_SIZE, PAGE_SIZE)
                 for j in range(PAGES_PER_STEP)])

    @pl.when(s == steps)
    def _():
        acc = acc_ref[...]
        row = lax.broadcasted_iota(jnp.int32, acc.shape, 0)
        col = lax.broadcasted_iota(jnp.int32, acc.shape, 1)
        own = (col // SB_HEAD_DIM) == row
        o_ref[...] = jnp.sum(jnp.where(own, acc, 0.0), axis=0, keepdims=True)


def sb_attention_sample(q, k_new, v_new, cache_k, cache_v, li, page_table, bias):
    bs, d = q.shape
    n_heads = d // SB_HEAD_DIM
    n_pages = page_table.shape[1]
    assert n_pages % PAGES_PER_STEP == 0 and cache_k.shape[2:] == (PAGE_SIZE, n_heads, SB_HEAD_DIM)
    qh = q.reshape(bs, n_heads, 1, SB_HEAD_DIM)
    eye = jnp.eye(n_heads, dtype=F32).reshape(1, n_heads, n_heads, 1)
    qbd = (qh * eye).reshape(bs, n_heads, d).astype(BF16)
    bias_b = jnp.broadcast_to(bias.astype(F32).reshape(n_heads, 1), (n_heads, PAGE_SIZE))
    page_buf = pltpu.VMEM((2, PAGES_PER_STEP, PAGE_SIZE, d), F32)
    grid_spec = pltpu.PrefetchScalarGridSpec(
        num_scalar_prefetch=1,
        grid=(bs, n_pages // PAGES_PER_STEP + 1),
        in_specs=[pl.BlockSpec((None, n_heads, d), lambda b, s, pt: (b, 0, 0)),
                  pl.BlockSpec((n_heads, PAGE_SIZE), lambda b, s, pt: (0, 0)),
                  pl.BlockSpec((LANES, 2 * LANES), lambda b, s, pt: (0, 0)),
                  pl.BlockSpec((None, 1, d), lambda b, s, pt: (b, 0, 0)),
                  pl.BlockSpec((None, 1, d), lambda b, s, pt: (b, 0, 0)),
                  pl.BlockSpec(memory_space=pl.ANY), pl.BlockSpec(memory_space=pl.ANY)],
        out_specs=pl.BlockSpec((None, 1, d), lambda b, s, pt: (b, 0, 0)),
        scratch_shapes=[page_buf, page_buf, pltpu.SemaphoreType.DMA((2,)),
                        pltpu.VMEM((n_heads, PAGE_SIZE), F32), pltpu.VMEM((n_heads, d), F32)],
    )
    out = pl.pallas_call(
        functools.partial(_sb_sample_kernel, li=li, n_pages=n_pages, n_new=1, scale=SB_HEAD_DIM ** -0.5),
        grid_spec=grid_spec,
        out_shape=jax.ShapeDtypeStruct((bs, 1, d), F32),
        compiler_params=_params("arbitrary", "arbitrary"),
        name="sb_attention_sample",
    )(page_table, qbd, bias_b, _suffix_matrix(), k_new.reshape(bs, 1, d), v_new.reshape(bs, 1, d), cache_k, cache_v)
    return out.reshape(bs, d)


def _head_minor_cols(w, n_heads):
    kdim = w.shape[0]
    return w.reshape(kdim, n_heads, RWKV_HEAD_DIM).swapaxes(1, 2).reshape(kdim, n_heads * RWKV_HEAD_DIM)


def _head_minor_vec(p, n_heads):
    return p.reshape(n_heads, RWKV_HEAD_DIM).T.reshape(1, n_heads * RWKV_HEAD_DIM).astype(F32)


def _mix_kernel(h_ref, xp_ref, mu_ref, *o_refs):
    h = h_ref[...]
    xx = xp_ref[...] - h
    for i, o in enumerate(o_refs):
        o[...] = (h + xx * mu_ref[i:i + 1, :]).astype(o.dtype)


def token_shift_mix(h, x_prev, mu, tm=128):
    m, d = h.shape
    tm = min(tm, m)
    assert m % tm == 0
    n_mix = mu.shape[0]
    row = pl.BlockSpec((tm, d), lambda i: (i, 0))
    return pl.pallas_call(
        _mix_kernel,
        grid=(m // tm,),
        in_specs=[row, row, pl.BlockSpec((n_mix, d), lambda i: (0, 0))],
        out_specs=[row] * n_mix,
        out_shape=[jax.ShapeDtypeStruct((m, d), BF16)] * n_mix,
        compiler_params=_params("parallel"),
        name="token_shift_mix",
    )(h, x_prev, mu.astype(F32))


def _norm_shift_mix_kernel(y_ref, yprev_ref, g_ref, shift_ref, mu_ref, *refs, t, tm):
    o_refs, hlast_ref = refs[:-1], refs[-1]
    i = pl.program_id(0)

    def norm(y):
        return (y * lax.rsqrt(jnp.mean(y * y, axis=-1, keepdims=True) + RMS_EPS)) * g_ref[...]

    h = norm(y_ref[...])
    above = norm(yprev_ref[SUBLANES - 1:SUBLANES, :])
    first = jnp.where((i * tm) % t == 0, shift_ref[...], above)
    row = lax.broadcasted_iota(jnp.int32, h.shape, 0)
    x_prev = jnp.where(row == 0, first, pltpu.roll(h, 1, 0))
    xx = x_prev - h
    for j, o in enumerate(o_refs):
        o[...] = (h + xx * mu_ref[j:j + 1, :]).astype(o.dtype)

    @pl.when(((i + 1) * tm) % t == 0)
    def _():
        hlast_ref[...] = h[tm - 1:tm, :]


def norm_shift_mix(y, g, shift_prev, mu, nb, t, tm=128):
    m, d = y.shape
    tm = min(tm, t)
    assert t % tm == 0 and tm % SUBLANES == 0 and m == nb * t
    n_mix = mu.shape[0]
    row = pl.BlockSpec((tm, d), lambda i: (i, 0))
    per_seq = pl.BlockSpec((None, 1, d), lambda i: ((i * tm) // t, 0, 0))
    outs = pl.pallas_call(
        functools.partial(_norm_shift_mix_kernel, t=t, tm=tm),
        grid=(m // tm,),
        in_specs=[row,
                  pl.BlockSpec((SUBLANES, d), lambda i: (jnp.maximum(i * (tm // SUBLANES) - 1, 0), 0)),
                  pl.BlockSpec((1, d), lambda i: (0, 0)),
                  per_seq,
                  pl.BlockSpec((n_mix, d), lambda i: (0, 0))],
        out_specs=[row] * n_mix + [per_seq],
        out_shape=[jax.ShapeDtypeStruct((m, d), BF16)] * n_mix + [jax.ShapeDtypeStruct((nb, 1, d), F32)],
        compiler_params=_params("arbitrary"),
        name="norm_shift_mix",
    )(y, y, g.reshape(1, d).astype(F32), shift_prev.reshape(nb, 1, d).astype(F32), mu.astype(F32))
    return outs[:n_mix], outs[n_mix].reshape(nb, d)


def _head_sum(x):
    s = x[:, 0:LANES]
    for m in range(1, x.shape[1] // LANES):
        s = s + x[:, m * LANES:(m + 1) * LANES]
    return s + pltpu.roll(s, LANES // 2, 1)


def _tile_lanes(s, d):
    return jnp.concatenate([s] * (d // LANES), axis=1)


def _rwkv_prep_kernel(r_ref, k_ref, tw_ref, ta_ref, w2_ref, a2_ref, prm_ref,
                      decay_ref, k2_ref, kneg_ref, bb_ref, rk_ref):
    d = r_ref.shape[1]
    w0, a0, k_k, k_a, r_k = (prm_ref[i:i + 1, :] for i in range(5))
    u = -(w0 + jnp.dot(tw_ref[...], w2_ref[...], preferred_element_type=F32))
    softplus = jnp.maximum(u, 0.0) + jnp.log(1.0 + jnp.exp(-jnp.abs(u)))
    decay_ref[...] = jnp.exp(-jnp.exp(-softplus - 0.5))
    a = 1.0 / (1.0 + jnp.exp(-(a0 + jnp.dot(ta_ref[...], a2_ref[...], preferred_element_type=F32))))
    k = k_ref[...]
    kk = k * k_k
    kk = kk * _tile_lanes(jnp.minimum(lax.rsqrt(_head_sum(kk * kk)), 1e12), d)
    k2 = k * (1.0 + (a - 1.0) * k_a)
    k2_ref[...] = k2
    kneg_ref[...] = -kk
    bb_ref[...] = kk * a
    rk_ref[...] = _head_sum(r_ref[...] * k2 * r_k)


def rwkv_prep(r, k, tw, ta, w2, a2, prm, tm=64):
    m, d = r.shape
    tm = min(tm, m)
    assert m % tm == 0
    row = pl.BlockSpec((tm, d), lambda i: (i, 0))
    lora = lambda x: pl.BlockSpec((tm, x.shape[1]), lambda i: (i, 0))
    whole = lambda x: pl.BlockSpec(x.shape, lambda i: (0, 0))
    full = jax.ShapeDtypeStruct((m, d), F32)
    return pl.pallas_call(
        _rwkv_prep_kernel,
        grid=(m // tm,),
        in_specs=[row, row, lora(tw), lora(ta), whole(w2), whole(a2), whole(prm)],
        out_specs=[row, row, row, row, pl.BlockSpec((tm, LANES), lambda i: (i, 0))],
        out_shape=[full, full, full, full, jax.ShapeDtypeStruct((m, LANES), F32)],
        compiler_params=_params("parallel"),
        name="rwkv_prep",
    )(r, k, tw, ta, w2, a2, prm)


N_PARTIAL_SUMS = 2


def _wkv_kernel(s0_ref, w_ref, k_ref, b_ref, a_ref, r_ref, v_ref, y_ref, st_ref, s_ref, *, tc, ns, nq, nm):
    c = pl.program_id(1)

    @pl.when(c == 0)
    def _():
        s_ref[...] = s0_ref[...]

    lane = lax.broadcasted_iota(jnp.int32, (SUBLANES, LANES), 1)
    lo_half = lane < (LANES // 2)
    swap = lambda x: pltpu.roll(x, LANES // 2, 1)
    rows = [(q, par) for q in range(nq) for par in range(2)]
    seqs = range(ns)

    def row(ref, e, t, m):
        return jnp.broadcast_to(ref[e, t, m:m + 1, :], (SUBLANES, LANES))

    def sum_partials(acc, key):
        tot = acc[key + (0,)]
        for i in range(1, min(N_PARTIAL_SUMS, nm)):
            tot = tot + acc[key + (i,)]
        return tot

    def step(t, _):
        sa_part = {}
        for e in seqs:
            for m in range(nm):
                a_m = row(a_ref, e, t, m)
                for qp in rows:
                    pr = s_ref[e, qp[0], qp[1], m] * a_m
                    key = (e, qp, m % N_PARTIAL_SUMS)
                    sa_part[key] = pr if key not in sa_part else sa_part[key] + pr
        sa, vt = {}, {}
        for e in seqs:
            for qp in rows:
                tot = sum_partials(sa_part, (e, qp))
                sa[(e, qp)] = tot + swap(tot)
            for q in range(nq):
                x = v_ref[e, t, q * SUBLANES:(q + 1) * SUBLANES, :]
                xs = swap(x)
                vt[(e, (q, 0))] = jnp.where(lo_half, x, xs)
                vt[(e, (q, 1))] = jnp.where(lo_half, xs, x)
        ya = {}
        for e in seqs:
            for m in range(nm):
                w_m, b_m, k_m, r_m = (row(ref, e, t, m) for ref in (w_ref, b_ref, k_ref, r_ref))
                for qp in rows:
                    s_new = s_ref[e, qp[0], qp[1], m] * w_m + sa[(e, qp)] * b_m + vt[(e, qp)] * k_m
                    s_ref[e, qp[0], qp[1], m] = s_new
                    pr = s_new * r_m
                    key = (e, qp, m % N_PARTIAL_SUMS)
                    ya[key] = pr if key not in ya else ya[key] + pr
        for e in seqs:
            for q in range(nq):
                ys = []
                for par in range(2):
                    tot = sum_partials(ya, (e, (q, par)))
                    ys.append(tot + swap(tot))
                y_ref[e, t, q * SUBLANES:(q + 1) * SUBLANES, :] = jnp.where(lo_half, ys[0], ys[1])
        return 0

    lax.fori_loop(0, tc, step, 0)

    @pl.when(c == pl.num_programs(1) - 1)
    def _():
        st_ref[...] = s_ref[...]


def wkv_scan(s0, w, k, b, a, r, v, nb, t):
    d = w.shape[1]
    nq, nm = s0.shape[1], s0.shape[3]
    ns = 2 if nb % 2 == 0 else 1
    tc = 32 if t % 32 == 0 else t
    assert t % tc == 0 and (tc % SUBLANES == 0 or tc == t)
    seq3 = lambda x: x.reshape(nb, t, d // LANES, LANES)
    col3 = pl.BlockSpec((ns, tc, d // LANES, LANES), lambda bi, c: (bi, c, 0, 0))
    st = pl.BlockSpec((ns,) + s0.shape[1:], lambda bi, c: (bi, 0, 0, 0, 0, 0))
    y, s_t = pl.pallas_call(
        functools.partial(_wkv_kernel, tc=tc, ns=ns, nq=nq, nm=nm),
        grid=(nb // ns, t // tc),
        in_specs=[st, col3, col3, col3, col3, col3, col3],
        out_specs=[col3, st],
        out_shape=[jax.ShapeDtypeStruct((nb, t, d // LANES, LANES), F32), jax.ShapeDtypeStruct(s0.shape, F32)],
        scratch_shapes=[pltpu.VMEM((ns,) + s0.shape[1:], F32)],
        compiler_params=_params("parallel", "arbitrary"),
        name="wkv_scan",
    )(s0, seq3(w), seq3(k), seq3(b), seq3(a), seq3(r), seq3(v))
    return y.reshape(nb * t, d), s_t


def _state_to_kernel(s):
    nb, nh, n, _ = s.shape
    x = s.reshape(nb, nh, n // 16, SUBLANES, 2, n // 2, 2)
    x = x.transpose(0, 2, 4, 5, 3, 6, 1)
    return x.reshape(nb, n // 16, 2, n // 2, SUBLANES, 2 * nh)


def _state_from_kernel(x, nh):
    nb, nq, _, nm, _, _ = x.shape
    x = x.reshape(nb, nq, 2, nm, SUBLANES, 2, nh)
    x = x.transpose(0, 6, 1, 4, 2, 3, 5)
    return x.reshape(nb, nh, nq * 16, nm * 2)


def _rwkv_post_kernel(y_ref, v_ref, sg_ref, g2_ref, rk_ref, ln_ref, o_ref):
    d = y_ref.shape[1]
    gate = jnp.dot(sg_ref[...], g2_ref[...], preferred_element_type=F32)
    n = RWKV_HEAD_DIM
    y = y_ref[...]
    mean = _tile_lanes(_head_sum(y), d) * (1.0 / n)
    yc = y - mean
    var = _tile_lanes(_head_sum(yc * yc), d) * (1.0 / n)
    yn = (yc * lax.rsqrt(var + GN_EPS)) * ln_ref[0:1, :] + ln_ref[1:2, :]
    bonus = _tile_lanes(rk_ref[...], d) * v_ref[...]
    o_ref[...] = ((yn + bonus) * gate).astype(o_ref.dtype)


def rwkv_post(y, v, sg, g2, rk, ln, tm=128):
    m, d = y.shape
    tm = min(tm, m)
    assert m % tm == 0
    row = pl.BlockSpec((tm, d), lambda i: (i, 0))
    return pl.pallas_call(
        _rwkv_post_kernel,
        grid=(m // tm,),
        in_specs=[row, row, pl.BlockSpec((tm, sg.shape[1]), lambda i: (i, 0)), pl.BlockSpec(g2.shape, lambda i: (0, 0)),
                  pl.BlockSpec((tm, LANES), lambda i: (i, 0)), pl.BlockSpec(ln.shape, lambda i: (0, 0))],
        out_specs=row,
        out_shape=jax.ShapeDtypeStruct((m, d), BF16),
        compiler_params=_params("parallel"),
        name="rwkv_post",
    )(y, v, sg, g2, rk, ln)


def normed_shift_mixes(y, g, shift_prev, mu, nb, t):
    d = y.shape[1]
    if t % LANES == 0:
        return norm_shift_mix(y, g, shift_prev, mu, nb, t)
    h = rmsnorm(y, g, F32)
    h3 = h.reshape(nb, t, d)
    x_prev = jnp.concatenate([shift_prev[:, None, :].astype(F32), h3[:, :-1]], axis=1).reshape(nb * t, d)
    return token_shift_mix(h, x_prev, mu), h3[:, -1, :]


def rwkv_time_mix(streams, wts):
    (xr, xw, xk, xv, xa, xg), (xr_s, xw_s, xk_s, xv_s, xa_s, xg_s) = [s[0] for s in streams]
    proj = lambda x, x_s, w, **kw: matmul(x, w, xs=x_s, **kw)
    r = proj(xr, xr_s, wts["w_r"])
    k = proj(xk, xk_s, wts["w_k"])
    v = proj(xv, xv_s, wts["w_v"])
    tw = proj(xw, xw_s, wts["w1"], act="tanh", out_dtypes=(BF16,))
    ta = proj(xa, xa_s, wts["a1"], out_dtypes=(BF16,))
    sg = proj(xg, xg_s, wts["g1"], act="sigmoid", out_dtypes=(BF16,))
    results = []
    for i, (_, s0, nb, t) in enumerate(streams):
        nh = s0.shape[1]
        decay, k2, kneg, bb, rk = rwkv_prep(r[i], k[i], tw[i], ta[i], wts["w2"], wts["a2"], wts["prm"])
        y, s_t = wkv_scan(_state_to_kernel(s0.astype(F32)), decay, k2, bb, kneg, r[i], v[i], nb, t)
        results.append((rwkv_post(y, v[i], sg[i], wts["g2"], rk, wts["ln"]), _state_from_kernel(s_t, nh)))
    return results


def kernel(x_prompt, x_sample, cache_sb_k, cache_sb_v, page_table, state_rwkv_wkv, state_rwkv_shift, norm_mix, norm_mlp, norm_final, sb_w_qkv, sb_w_o, sb_bias, rw_mu, rw_w_r, rw_w_k, rw_w_v, rw_w_o, rw_w0, rw_w1, rw_w2, rw_a0, rw_a1, rw_a2, rw_g1, rw_g2, rw_k_k, rw_k_a, rw_r_k, rw_ln_w, rw_ln_b, mlp_w1, mlp_w2):
    bp, tp, d = x_prompt.shape
    bs, ts, _ = x_sample.shape
    assert ts == 1, "the paged decode attention handles one new token per sequence"
    depth = norm_mix.shape[0]
    nh_rw = d // RWKV_HEAD_DIM
    n_sb_heads = d // SB_HEAD_DIM

    yp = x_prompt.reshape(bp * tp, d)
    ys = x_sample.reshape(bs * ts, d)
    kp_l, vp_l, ks_l, vs_l = [], [], [], []
    wkvp_l, shp_l, wkvs_l, shs_l = [], [], [], []

    w2_stack = mlp_w2.astype(BF16)
    for layer in range(depth):
        li = layer // 2
        if layer % 2 == 0:
            bias = sb_bias[li]
            hp = rmsnorm(yp, norm_mix[layer], BF16)
            hs = rmsnorm(ys, norm_mix[layer], BF16)
            qp, qs = matmul(hp, sb_w_qkv, layer=li, n_off=0, n=d, xs=hs, out_dtypes=(BF16,))
            (kp, kp16), (ks, _) = matmul(hp, sb_w_qkv, layer=li, n_off=d, n=d, xs=hs, out_dtypes=(F32, BF16))
            (vp, vp16), (vs, _) = matmul(hp, sb_w_qkv, layer=li, n_off=2 * d, n=d, xs=hs, out_dtypes=(F32, BF16))
            op = sb_attention_prompt(qp, kp16, vp16, bias, bp, tp)
            os_ = sb_attention_sample(qs, ks, vs, cache_sb_k, cache_sb_v, li, page_table, bias)
            yp, ys = matmul(op, sb_w_o, layer=li, res=yp, xs=os_.astype(BF16), res_s=ys)
            kp_l.append(kp.reshape(bp, tp, n_sb_heads, SB_HEAD_DIM))
            vp_l.append(vp.reshape(bp, tp, n_sb_heads, SB_HEAD_DIM))
            ks_l.append(ks.reshape(bs, ts, n_sb_heads, SB_HEAD_DIM))
            vs_l.append(vs.reshape(bs, ts, n_sb_heads, SB_HEAD_DIM))
        else:
            hm = lambda w: _head_minor_cols(w.astype(BF16), nh_rw)
            g_pad = -rw_g1.shape[-1] % LANES
            wts = {
                "w_r": hm(rw_w_r[li]), "w_k": hm(rw_w_k[li]), "w_v": hm(rw_w_v[li]),
                "w1": rw_w1[li].astype(BF16), "w2": hm(rw_w2[li]),
                "a1": rw_a1[li].astype(BF16), "a2": hm(rw_a2[li]),
                "g1": jnp.pad(rw_g1[li], ((0, 0), (0, g_pad))).astype(BF16),
                "g2": hm(jnp.pad(rw_g2[li], ((0, g_pad), (0, 0)))),
                "prm": jnp.concatenate([_head_minor_vec(p, nh_rw) for p in
                                        (rw_w0[li], rw_a0[li], rw_k_k[li], rw_k_a[li], rw_r_k[li])]
                                       + [jnp.zeros((3, d), F32)], axis=0),
                "ln": jnp.concatenate([_head_minor_vec(rw_ln_w[li], nh_rw), _head_minor_vec(rw_ln_b[li], nh_rw)], axis=0),
            }
            w_o = rw_w_o[li].astype(BF16).reshape(nh_rw, RWKV_HEAD_DIM, d).swapaxes(0, 1).reshape(d, d)
            zero_shift = jnp.zeros((bp, d), F32)
            zero_wkv = jnp.zeros((bp, nh_rw, RWKV_HEAD_DIM, RWKV_HEAD_DIM), F32)
            mix_p, shp = normed_shift_mixes(yp, norm_mix[layer], zero_shift, rw_mu[li], bp, tp)
            mix_s, shs = normed_shift_mixes(ys, norm_mix[layer], state_rwkv_shift[li], rw_mu[li], bs, ts)
            (op, wkvp), (os_, wkvs) = rwkv_time_mix(
                [(mix_p, zero_wkv, bp, tp), (mix_s, state_rwkv_wkv[li], bs, ts)], wts)
            yp, ys = matmul(op, w_o, res=yp, xs=os_, res_s=ys)
            wkvp_l.append(wkvp)
            shp_l.append(shp)
            wkvs_l.append(wkvs)
            shs_l.append(shs)
        hid_p, hid_s = matmul(rmsnorm(yp, norm_mlp[layer], BF16), mlp_w1, layer=layer, act="relu2",
                              xs=rmsnorm(ys, norm_mlp[layer], BF16), out_dtypes=(BF16,))
        yp, ys = matmul(hid_p, w2_stack, layer=layer, res=yp, xs=hid_s, res_s=ys, tk=4096)

    y_prompt = rmsnorm(yp, norm_final, F32).reshape(bp, tp, d)
    y_sample = rmsnorm(ys, norm_final, F32).reshape(bs, ts, d)
    return (y_prompt, y_sample, jnp.stack(kp_l), jnp.stack(vp_l), jnp.stack(ks_l), jnp.stack(vs_l),
            jnp.stack(wkvp_l), jnp.stack(shp_l), jnp.stack(wkvs_l), jnp.stack(shs_l))
```

```python
import functools

import jax
import jax.numpy as jnp
from jax import lax
from jax.experimental import pallas as pl
from jax.experimental.pallas import tpu as pltpu

F32 = jnp.float32
BF16 = jnp.bfloat16

LANES = 128
SUBLANES = 8
VMEM_LIMIT_BYTES = 56 * 1024 * 1024

PAGE_SIZE = 128
SB_HEAD_DIM = 128
RWKV_HEAD_DIM = 64
RMS_EPS = 1e-5
GN_EPS = 64e-5


def _params(*sem):
    return pltpu.CompilerParams(dimension_semantics=sem, vmem_limit_bytes=VMEM_LIMIT_BYTES)


def _rmsnorm_kernel(x_ref, g_ref, o_ref):
    x = x_ref[...]
    ms = jnp.mean(x * x, axis=-1, keepdims=True)
    o_ref[...] = ((x * lax.rsqrt(ms + RMS_EPS)) * g_ref[...]).astype(o_ref.dtype)


def rmsnorm(x, g, out_dtype, tm=512):
    m, d = x.shape
    tm = min(tm, m)
    assert m % tm == 0
    return pl.pallas_call(
        _rmsnorm_kernel,
        grid=(m // tm,),
        in_specs=[pl.BlockSpec((tm, d), lambda i: (i, 0)), pl.BlockSpec((1, d), lambda i: (0, 0))],
        out_specs=pl.BlockSpec((tm, d), lambda i: (i, 0)),
        out_shape=jax.ShapeDtypeStruct((m, d), out_dtype),
        compiler_params=_params("parallel"),
        name="rmsnorm",
    )(x, g.reshape(1, d).astype(F32))


def _mm_kernel(*refs, nk, act, has_res, n_out, has_s, w_f32):
    it = iter(refs)
    x_ref, w_ref = next(it), next(it)
    res_ref = next(it) if has_res else None
    xs_ref = next(it) if has_s else None
    ress_ref = next(it) if (has_s and has_res) else None
    out_refs = [next(it) for _ in range(n_out)]
    outs_refs = [next(it) for _ in range(n_out)] if has_s else []
    wb_ref = next(it) if w_f32 else None
    acc_ref = next(it) if nk > 1 else None
    accs_ref = next(it) if (nk > 1 and has_s) else None
    i, k = pl.program_id(1), pl.program_id(2)

    if w_f32:
        @pl.when(i == 0)
        def _():
            wb_ref[...] = w_ref[...].astype(BF16)
        w_ref = wb_ref

    def finish(acc, r_ref, o_refs):
        if act == "relu2":
            r = jnp.maximum(acc, 0.0)
            acc = r * r
        elif act == "tanh":
            acc = jnp.tanh(acc)
        elif act == "sigmoid":
            acc = 1.0 / (1.0 + jnp.exp(-acc))
        if has_res:
            acc = r_ref[...] + acc
        for o in o_refs:
            o[...] = acc.astype(o.dtype)

    def run(lhs_ref, r_ref, o_refs, a_ref):
        part = jnp.dot(lhs_ref[...], w_ref[...], preferred_element_type=F32)
        if nk == 1:
            finish(part, r_ref, o_refs)
            return

        @pl.when(k == 0)
        def _():
            a_ref[...] = part

        @pl.when(k > 0)
        def _():
            a_ref[...] += part

        @pl.when(k == nk - 1)
        def _():
            finish(a_ref[...], r_ref, o_refs)

    run(x_ref, res_ref, out_refs, acc_ref)
    if has_s:
        @pl.when(i == 0)
        def _():
            run(xs_ref, ress_ref, outs_refs, accs_ref)


def matmul(x, w, *, layer=None, n_off=0, n=None, act=None, res=None, xs=None, res_s=None,
           out_dtypes=(F32,), tm=1024, tn=512, tk=None):
    m, kdim = x.shape
    w_f32 = w.dtype == F32
    assert (layer is not None) == (w.ndim == 3)
    n = w.shape[-1] if n is None else n
    tm, tn = min(tm, m), min(tn, n)
    tk = kdim if tk is None else min(tk, kdim)
    assert m % tm == 0 and n % tn == 0 and kdim % tk == 0 and n_off % tn == 0 and w.shape[-2] == kdim
    assert not (w_f32 and tk != kdim) and (w_f32 or w.dtype == BF16)
    assert (res_s is None) == (res is None or xs is None)
    nk, joff = kdim // tk, n_off // tn
    has_s, has_res = xs is not None, res is not None
    if layer is not None:
        w_spec = pl.BlockSpec((None, tk, tn), lambda j, i, k: (layer, k, j + joff))
    else:
        w_spec = pl.BlockSpec((tk, tn), lambda j, i, k: (k, j + joff))
    in_specs, args = [pl.BlockSpec((tm, tk), lambda j, i, k: (i, k)), w_spec], [x, w]
    if has_res:
        in_specs.append(pl.BlockSpec((tm, tn), lambda j, i, k: (i, j)))
        args.append(res)
    out_specs = [pl.BlockSpec((tm, tn), lambda j, i, k: (i, j)) for _ in out_dtypes]
    out_shape = [jax.ShapeDtypeStruct((m, n), dt) for dt in out_dtypes]
    scratch = [pltpu.VMEM((tk, tn), BF16)] if w_f32 else []
    if nk > 1:
        scratch.append(pltpu.VMEM((tm, tn), F32))
    if has_s:
        ms = xs.shape[0]
        in_specs.append(pl.BlockSpec((ms, tk), lambda j, i, k: (0, k)))
        args.append(xs)
        if has_res:
            in_specs.append(pl.BlockSpec((ms, tn), lambda j, i, k: (0, j)))
            args.append(res_s)
        out_specs += [pl.BlockSpec((ms, tn), lambda j, i, k: (0, j)) for _ in out_dtypes]
        out_shape += [jax.ShapeDtypeStruct((ms, n), dt) for dt in out_dtypes]
        if nk > 1:
            scratch.append(pltpu.VMEM((ms, tn), F32))
    out = pl.pallas_call(
        functools.partial(_mm_kernel, nk=nk, act=act, has_res=has_res, n_out=len(out_dtypes), has_s=has_s,
                          w_f32=w_f32),
        grid=(n // tn, m // tm, nk),
        in_specs=in_specs,
        out_specs=out_specs,
        out_shape=out_shape,
        scratch_shapes=scratch,
        compiler_params=_params("parallel", "arbitrary", "arbitrary"),
        name="matmul_f32w" if w_f32 else "matmul",
    )(*args)
    unpack = lambda o: o[0] if len(out_dtypes) == 1 else tuple(o)
    if has_s:
        return unpack(out[:len(out_dtypes)]), unpack(out[len(out_dtypes):])
    return unpack(out)


def _log_sigmoid_pair(z):
    t = jnp.log(1.0 + jnp.exp(-jnp.abs(z)))
    return jnp.minimum(z, 0.0) - t, jnp.minimum(-z, 0.0) - t


def _suffix_sums(log_rest, u, two_terms=True):
    hi = log_rest.astype(BF16)
    if not two_terms:
        return jnp.dot(hi, u, preferred_element_type=F32)
    lo = (log_rest - hi.astype(F32)).astype(BF16)
    return (jnp.dot(hi, u, preferred_element_type=F32) + jnp.dot(lo, u, preferred_element_type=F32))


def _sb_prompt_kernel(bias_ref, q_ref, k_ref, v_ref, u_ref, o_ref, *, bq, hp, scale):
    hd = SB_HEAD_DIM
    h0 = pl.program_id(1) * hp
    qi = pl.program_id(2)
    u = u_ref[...]
    heads = range(hp)
    qs = [q_ref[:, hh * hd:(hh + 1) * hd] for hh in heads]
    biases = [bias_ref[h0 + hh] for hh in heads]
    n_chunks = bq // LANES

    def group(g, carries, accs, masked):
        start = pl.multiple_of(g * bq, bq)
        zs = []
        for hh in heads:
            k = k_ref[pl.ds(start, bq), hh * hd:(hh + 1) * hd]
            zs.append(lax.dot_general(qs[hh], k, (((1,), (1,)), ((), ())), preferred_element_type=F32)
                      * scale + biases[hh])
        if masked:
            row = lax.broadcasted_iota(jnp.int32, (bq, bq), 0)
            col = lax.broadcasted_iota(jnp.int32, (bq, bq), 1)
            mask = col < row
        ls_pos, ls_neg = [], []
        for hh in heads:
            lp, ln = _log_sigmoid_pair(zs[hh])
            ls_pos.append(lp)
            ls_neg.append(jnp.where(mask, ln, 0.0) if masked else ln)
        cs = [[_suffix_sums(ls_neg[hh][:, c * LANES:(c + 1) * LANES], u, two_terms=False)
               for c in range(n_chunks)] for hh in heads]
        new_carries, new_accs = [], []
        for hh in heads:
            carry = carries[hh]
            between = [None] * n_chunks
            for c in reversed(range(n_chunks)):
                between[c] = cs[hh][c][:, :LANES] + carry
                carry = carry + cs[hh][c][:, LANES:]
            p = jnp.exp(ls_pos[hh] + jnp.concatenate(between, axis=1))
            if masked:
                p = jnp.where(mask, p, 0.0)
            v = v_ref[pl.ds(start, bq), hh * hd:(hh + 1) * hd]
            new_accs.append(accs[hh] + jnp.dot(p.astype(BF16), v, preferred_element_type=F32))
            new_carries.append(carry)
        return new_carries, new_accs

    carries = [jnp.zeros((bq, LANES), F32) for _ in heads]
    accs = [jnp.zeros((bq, hd), F32) for _ in heads]
    carries, accs = group(qi, carries, accs, True)

    def body(i, ca):
        c, a = group(qi - 1 - i, list(ca[0]), list(ca[1]), False)
        return tuple(c), tuple(a)

    carries, accs = lax.fori_loop(0, qi, body, (tuple(carries), tuple(accs)))
    for hh in heads:
        o_ref[:, hh * hd:(hh + 1) * hd] = accs[hh].astype(o_ref.dtype)


def _suffix_matrix():
    j = lax.broadcasted_iota(jnp.int32, (LANES, 2 * LANES), 0)
    s = lax.broadcasted_iota(jnp.int32, (LANES, 2 * LANES), 1)
    return jnp.where((s >= LANES) | (j > s), 1.0, 0.0).astype(BF16)


def sb_attention_prompt(q, k, v, bias, b, t):
    n_heads = q.shape[1] // SB_HEAD_DIM
    hp = 4 if n_heads % 4 == 0 else 1
    bq = 256 if t % 256 == 0 else LANES
    assert t % bq == 0
    nq = t // bq
    wide = hp * SB_HEAD_DIM
    grid_spec = pltpu.PrefetchScalarGridSpec(
        num_scalar_prefetch=1,
        grid=(b, n_heads // hp, nq),
        in_specs=[pl.BlockSpec((bq, wide), lambda bi, h, i, _: (bi * nq + i, h)),
                  pl.BlockSpec((t, wide), lambda bi, h, i, _: (bi, h)),
                  pl.BlockSpec((t, wide), lambda bi, h, i, _: (bi, h)),
                  pl.BlockSpec((LANES, 2 * LANES), lambda bi, h, i, _: (0, 0))],
        out_specs=pl.BlockSpec((bq, wide), lambda bi, h, i, _: (bi * nq + i, h)),
    )
    return pl.pallas_call(
        functools.partial(_sb_prompt_kernel, bq=bq, hp=hp, scale=SB_HEAD_DIM ** -0.5),
        grid_spec=grid_spec,
        out_shape=jax.ShapeDtypeStruct(q.shape, BF16),
        compiler_params=_params("parallel", "parallel", "parallel"),
        name="sb_attention_prompt",
    )(bias.astype(F32), q, k, v, _suffix_matrix())


PAGES_PER_STEP = 2


def _sb_sample_kernel(pt_ref, qbd_ref, bias_ref, u_ref, knew_ref, vnew_ref, kc_hbm, vc_hbm,
                      o_ref, kbuf, vbuf, sem, carry_ref, acc_ref, *, li, n_pages, n_new, scale):
    b = pl.program_id(0)
    s = pl.program_id(1)
    n_seq = pl.num_programs(0)
    n_heads = qbd_ref.shape[0]
    steps = n_pages // PAGES_PER_STEP
    q_pos = n_pages * PAGE_SIZE
    qbd = qbd_ref[...]

    def page_copies(seq, step, slot):
        cps = []
        for j in range(PAGES_PER_STEP):
            page = pt_ref[seq, n_pages - PAGES_PER_STEP * step + (PAGES_PER_STEP - 1 - j)]
            for h in range(n_heads):
                win = pl.ds(h * SB_HEAD_DIM, SB_HEAD_DIM)
                cps.append(pltpu.make_async_copy(kc_hbm.at[li, page, :, h, :], kbuf.at[slot, j, :, win], sem.at[slot]))
                cps.append(pltpu.make_async_copy(vc_hbm.at[li, page, :, h, :], vbuf.at[slot, j, :, win], sem.at[slot]))
        return cps

    @pl.when((b == 0) & (s == 0))
    def _():
        for cp in page_copies(0, 1, 0):
            cp.start()

    @pl.when(s == 0)
    def _():
        carry_ref[...] = jnp.zeros_like(carry_ref)
        acc_ref[...] = jnp.zeros_like(acc_ref)

    def process(pages):
        zs = [lax.dot_general(qbd, k_page.astype(BF16), (((1,), (1,)), ((), ())), preferred_element_type=F32)
              * scale + bias_ref[...] for k_page, _, _, _ in pages]
        idx = lax.broadcasted_iota(jnp.int32, zs[0].shape, 1)
        masks = [((pos0 + idx) < q_pos) & (idx < n_valid) for _, _, pos0, n_valid in pages]
        ls = [_log_sigmoid_pair(z) for z in zs]
        cs = [_suffix_sums(jnp.where(m, ln, 0.0), u_ref[...]) for m, (_, ln) in zip(masks, ls)]
        carry = carry_ref[...]
        atts = []
        for m, (lp, _), c in zip(masks, ls, cs):
            atts.append(jnp.where(m, jnp.exp(lp + (c[:, :PAGE_SIZE] + carry)), 0.0).astype(BF16))
            carry = carry + c[:, PAGE_SIZE:]
        carry_ref[...] = carry
        acc = acc_ref[...]
        for att, (_, v_page, _, _) in zip(atts, pages):
            acc = acc + jnp.dot(att, v_page.astype(BF16), preferred_element_type=F32)
        acc_ref[...] = acc

    @pl.when(s == 0)
    def _():
        first_row = lax.broadcasted_iota(jnp.int32, (PAGE_SIZE, qbd.shape[1]), 0) < n_new
        process([(jnp.where(first_row, knew_ref[...], 0.0), jnp.where(first_row, vnew_ref[...], 0.0), q_pos, n_new)])

    @pl.when(s > 0)
    def _():
        slot = (b * steps + s - 1) % 2

        @pl.when(s < steps)
        def _():
            for cp in page_copies(b, s + 1, 1 - slot):
                cp.start()

        @pl.when((s == steps) & (b + 1 < n_seq))
        def _():
            for cp in page_copies(b + 1, 1, 1 - slot):
                cp.start()

        for cp in page_copies(b, s, slot):
            cp.wait()
        first = (n_pages - PAGES_PER_STEP * s) * PAGE_SIZE
        process([(kbuf[slot, j], vbuf[slot, j], first + (PAGES_PER_STEP - 1 - j) * PAGE_SIZE, PAGE_SIZE)
                 for j in range(PAGES_PER_STEP)])

    @pl.when(s == steps)
    def _():
        acc = acc_ref[...]
        row = lax.broadcasted_iota(jnp.int32, acc.shape, 0)
        col = lax.broadcasted_iota(jnp.int32, acc.shape, 1)
        own = (col // SB_HEAD_DIM) == row
        o_ref[...] = jnp.sum(jnp.where(own, acc, 0.0), axis=0, keepdims=True)


def sb_attention_sample(q, k_new, v_new, cache_k, cache_v, li, page_table, bias):
    bs, d = q.shape
    n_heads = d // SB_HEAD_DIM
    n_pages = page_table.shape[1]
    assert n_pages % PAGES_PER_STEP == 0 and cache_k.shape[2:] == (PAGE_SIZE, n_heads, SB_HEAD_DIM)
    qh = q.reshape(bs, n_heads, 1, SB_HEAD_DIM)
    eye = jnp.eye(n_heads, dtype=F32).reshape(1, n_heads, n_heads, 1)
    qbd = (qh * eye).reshape(bs, n_heads, d).astype(BF16)
    bias_b = jnp.broadcast_to(bias.astype(F32).reshape(n_heads, 1), (n_heads, PAGE_SIZE))
    page_buf = pltpu.VMEM((2, PAGES_PER_STEP, PAGE_SIZE, d), F32)
    grid_spec = pltpu.PrefetchScalarGridSpec(
        num_scalar_prefetch=1,
        grid=(bs, n_pages // PAGES_PER_STEP + 1),
        in_specs=[pl.BlockSpec((None, n_heads, d), lambda b, s, pt: (b, 0, 0)),
                  pl.BlockSpec((n_heads, PAGE_SIZE), lambda b, s, pt: (0, 0)),
                  pl.BlockSpec((LANES, 2 * LANES), lambda b, s, pt: (0, 0)),
                  pl.BlockSpec((None, 1, d), lambda b, s, pt: (b, 0, 0)),
                  pl.BlockSpec((None, 1, d), lambda b, s, pt: (b, 0, 0)),
                  pl.BlockSpec(memory_space=pl.ANY), pl.BlockSpec(memory_space=pl.ANY)],
        out_specs=pl.BlockSpec((None, 1, d), lambda b, s, pt: (b, 0, 0)),
        scratch_shapes=[page_buf, page_buf, pltpu.SemaphoreType.DMA((2,)),
                        pltpu.VMEM((n_heads, PAGE_SIZE), F32), pltpu.VMEM((n_heads, d), F32)],
    )
    out = pl.pallas_call(
        functools.partial(_sb_sample_kernel, li=li, n_pages=n_pages, n_new=1, scale=SB_HEAD_DIM ** -0.5),
        grid_spec=grid_spec,
        out_shape=jax.ShapeDtypeStruct((bs, 1, d), F32),
        compiler_params=_params("arbitrary", "arbitrary"),
        name="sb_attention_sample",
    )(page_table, qbd, bias_b, _suffix_matrix(), k_new.reshape(bs, 1, d), v_new.reshape(bs, 1, d), cache_k, cache_v)
    return out.reshape(bs, d)


def _head_minor_cols(w, n_heads):
    kdim = w.shape[0]
    return w.reshape(kdim, n_heads, RWKV_HEAD_DIM).swapaxes(1, 2).reshape(kdim, n_heads * RWKV_HEAD_DIM)


def _head_minor_vec(p, n_heads):
    return p.reshape(n_heads, RWKV_HEAD_DIM).T.reshape(1, n_heads * RWKV_HEAD_DIM).astype(F32)


def _mix_kernel(h_ref, xp_ref, mu_ref, *o_refs):
    h = h_ref[...]
    xx = xp_ref[...] - h
    for i, o in enumerate(o_refs):
        o[...] = (h + xx * mu_ref[i:i + 1, :]).astype(o.dtype)


def token_shift_mix(h, x_prev, mu, tm=128):
    m, d = h.shape
    tm = min(tm, m)
    assert m % tm == 0
    n_mix = mu.shape[0]
    row = pl.BlockSpec((tm, d), lambda i: (i, 0))
    return pl.pallas_call(
        _mix_kernel,
        grid=(m // tm,),
        in_specs=[row, row, pl.BlockSpec((n_mix, d), lambda i: (0, 0))],
        out_specs=[row] * n_mix,
        out_shape=[jax.ShapeDtypeStruct((m, d), BF16)] * n_mix,
        compiler_params=_params("parallel"),
        name="token_shift_mix",
    )(h, x_prev, mu.astype(F32))


def _norm_shift_mix_kernel(y_ref, yprev_ref, g_ref, shift_ref, mu_ref, *refs, t, tm):
    o_refs, hlast_ref = refs[:-1], refs[-1]
    i = pl.program_id(0)

    def norm(y):
        return (y * lax.rsqrt(jnp.mean(y * y, axis=-1, keepdims=True) + RMS_EPS)) * g_ref[...]

    h = norm(y_ref[...])
    above = norm(yprev_ref[SUBLANES - 1:SUBLANES, :])
    first = jnp.where((i * tm) % t == 0, shift_ref[...], above)
    row = lax.broadcasted_iota(jnp.int32, h.shape, 0)
    x_prev = jnp.where(row == 0, first, pltpu.roll(h, 1, 0))
    xx = x_prev - h
    for j, o in enumerate(o_refs):
        o[...] = (h + xx * mu_ref[j:j + 1, :]).astype(o.dtype)

    @pl.when(((i + 1) * tm) % t == 0)
    def _():
        hlast_ref[...] = h[tm - 1:tm, :]


def norm_shift_mix(y, g, shift_prev, mu, nb, t, tm=128):
    m, d = y.shape
    tm = min(tm, t)
    assert t % tm == 0 and tm % SUBLANES == 0 and m == nb * t
    n_mix = mu.shape[0]
    row = pl.BlockSpec((tm, d), lambda i: (i, 0))
    per_seq = pl.BlockSpec((None, 1, d), lambda i: ((i * tm) // t, 0, 0))
    outs = pl.pallas_call(
        functools.partial(_norm_shift_mix_kernel, t=t, tm=tm),
        grid=(m // tm,),
        in_specs=[row,
                  pl.BlockSpec((SUBLANES, d), lambda i: (jnp.maximum(i * (tm // SUBLANES) - 1, 0), 0)),
                  pl.BlockSpec((1, d), lambda i: (0, 0)),
                  per_seq,
                  pl.BlockSpec((n_mix, d), lambda i: (0, 0))],
        out_specs=[row] * n_mix + [per_seq],
        out_shape=[jax.ShapeDtypeStruct((m, d), BF16)] * n_mix + [jax.ShapeDtypeStruct((nb, 1, d), F32)],
        compiler_params=_params("arbitrary"),
        name="norm_shift_mix",
    )(y, y, g.reshape(1, d).astype(F32), shift_prev.reshape(nb, 1, d).astype(F32), mu.astype(F32))
    return outs[:n_mix], outs[n_mix].reshape(nb, d)


def _head_sum(x):
    s = x[:, 0:LANES]
    for m in range(1, x.shape[1] // LANES):
        s = s + x[:, m * LANES:(m + 1) * LANES]
    return s + pltpu.roll(s, LANES // 2, 1)


def _tile_lanes(s, d):
    return jnp.concatenate([s] * (d // LANES), axis=1)


def _rwkv_prep_kernel(r_ref, k_ref, tw_ref, ta_ref, w2_ref, a2_ref, prm_ref,
                      decay_ref, k2_ref, kneg_ref, bb_ref, rk_ref):
    d = r_ref.shape[1]
    w0, a0, k_k, k_a, r_k = (prm_ref[i:i + 1, :] for i in range(5))
    u = -(w0 + jnp.dot(tw_ref[...], w2_ref[...], preferred_element_type=F32))
    softplus = jnp.maximum(u, 0.0) + jnp.log(1.0 + jnp.exp(-jnp.abs(u)))
    decay_ref[...] = jnp.exp(-jnp.exp(-softplus - 0.5))
    a = 1.0 / (1.0 + jnp.exp(-(a0 + jnp.dot(ta_ref[...], a2_ref[...], preferred_element_type=F32))))
    k = k_ref[...]
    kk = k * k_k
    kk = kk * _tile_lanes(jnp.minimum(lax.rsqrt(_head_sum(kk * kk)), 1e12), d)
    k2 = k * (1.0 + (a - 1.0) * k_a)
    k2_ref[...] = k2
    kneg_ref[...] = -kk
    bb_ref[...] = kk * a
    rk_ref[...] = _head_sum(r_ref[...] * k2 * r_k)


def rwkv_prep(r, k, tw, ta, w2, a2, prm, tm=128):
    m, d = r.shape
    tm = min(tm, m)
    assert m % tm == 0
    row = pl.BlockSpec((tm, d), lambda i: (i, 0))
    lora = lambda x: pl.BlockSpec((tm, x.shape[1]), lambda i: (i, 0))
    whole = lambda x: pl.BlockSpec(x.shape, lambda i: (0, 0))
    full = jax.ShapeDtypeStruct((m, d), F32)
    return pl.pallas_call(
        _rwkv_prep_kernel,
        grid=(m // tm,),
        in_specs=[row, row, lora(tw), lora(ta), whole(w2), whole(a2), whole(prm)],
        out_specs=[row, row, row, row, pl.BlockSpec((tm, LANES), lambda i: (i, 0))],
        out_shape=[full, full, full, full, jax.ShapeDtypeStruct((m, LANES), F32)],
        compiler_params=_params("parallel"),
        name="rwkv_prep",
    )(r, k, tw, ta, w2, a2, prm)


N_PARTIAL_SUMS = 2


def _wkv_kernel(s0_ref, w_ref, k_ref, b_ref, a_ref, r_ref, v_ref, y_ref, st_ref, s_ref, *, tc, ns, nq, nm):
    c = pl.program_id(1)

    @pl.when(c == 0)
    def _():
        s_ref[...] = s0_ref[...]

    lane = lax.broadcasted_iota(jnp.int32, (SUBLANES, LANES), 1)
    lo_half = lane < (LANES // 2)
    swap = lambda x: pltpu.roll(x, LANES // 2, 1)
    rows = [(q, par) for q in range(nq) for par in range(2)]
    seqs = range(ns)

    def row(ref, e, t, m):
        return jnp.broadcast_to(ref[e, t, m:m + 1, :], (SUBLANES, LANES))

    def sum_partials(acc, key):
        tot = acc[key + (0,)]
        for i in range(1, min(N_PARTIAL_SUMS, nm)):
            tot = tot + acc[key + (i,)]
        return tot

    def step(t, _):
        sa_part = {}
        for e in seqs:
            for m in range(nm):
                a_m = row(a_ref, e, t, m)
                for qp in rows:
                    pr = s_ref[e, qp[0], qp[1], m] * a_m
                    key = (e, qp, m % N_PARTIAL_SUMS)
                    sa_part[key] = pr if key not in sa_part else sa_part[key] + pr
        sa, vt = {}, {}
        for e in seqs:
            for qp in rows:
                tot = sum_partials(sa_part, (e, qp))
                sa[(e, qp)] = tot + swap(tot)
            for q in range(nq):
                x = v_ref[e, t, q * SUBLANES:(q + 1) * SUBLANES, :]
                xs = swap(x)
                vt[(e, (q, 0))] = jnp.where(lo_half, x, xs)
                vt[(e, (q, 1))] = jnp.where(lo_half, xs, x)
        ya = {}
        for e in seqs:
            for m in range(nm):
                w_m, b_m, k_m, r_m = (row(ref, e, t, m) for ref in (w_ref, b_ref, k_ref, r_ref))
                for qp in rows:
                    s_new = s_ref[e, qp[0], qp[1], m] * w_m + sa[(e, qp)] * b_m + vt[(e, qp)] * k_m
                    s_ref[e, qp[0], qp[1], m] = s_new
                    pr = s_new * r_m
                    key = (e, qp, m % N_PARTIAL_SUMS)
                    ya[key] = pr if key not in ya else ya[key] + pr
        for e in seqs:
            for q in range(nq):
                ys = []
                for par in range(2):
                    tot = sum_partials(ya, (e, (q, par)))
                    ys.append(tot + swap(tot))
                y_ref[e, t, q * SUBLANES:(q + 1) * SUBLANES, :] = jnp.where(lo_half, ys[0], ys[1])
        return 0

    lax.fori_loop(0, tc, step, 0)

    @pl.when(c == pl.num_programs(1) - 1)
    def _():
        st_ref[...] = s_ref[...]


def wkv_scan(s0, w, k, b, a, r, v, nb, t):
    d = w.shape[1]
    nq, nm = s0.shape[1], s0.shape[3]
    ns = 2 if nb % 2 == 0 else 1
    tc = 64 if t % 64 == 0 else t
    assert t % tc == 0 and (tc % SUBLANES == 0 or tc == t)
    seq3 = lambda x: x.reshape(nb, t, d // LANES, LANES)
    col3 = pl.BlockSpec((ns, tc, d // LANES, LANES), lambda bi, c: (bi, c, 0, 0))
    st = pl.BlockSpec((ns,) + s0.shape[1:], lambda bi, c: (bi, 0, 0, 0, 0, 0))
    y, s_t = pl.pallas_call(
        functools.partial(_wkv_kernel, tc=tc, ns=ns, nq=nq, nm=nm),
        grid=(nb // ns, t // tc),
        in_specs=[st, col3, col3, col3, col3, col3, col3],
        out_specs=[col3, st],
        out_shape=[jax.ShapeDtypeStruct((nb, t, d // LANES, LANES), F32), jax.ShapeDtypeStruct(s0.shape, F32)],
        scratch_shapes=[pltpu.VMEM((ns,) + s0.shape[1:], F32)],
        compiler_params=_params("parallel", "arbitrary"),
        name="wkv_scan",
    )(s0, seq3(w), seq3(k), seq3(b), seq3(a), seq3(r), seq3(v))
    return y.reshape(nb * t, d), s_t


def _state_to_kernel(s):
    nb, nh, n, _ = s.shape
    x = s.reshape(nb, nh, n // 16, SUBLANES, 2, n // 2, 2)
    x = x.transpose(0, 2, 4, 5, 3, 6, 1)
    return x.reshape(nb, n // 16, 2, n // 2, SUBLANES, 2 * nh)


def _state_from_kernel(x, nh):
    nb, nq, _, nm, _, _ = x.shape
    x = x.reshape(nb, nq, 2, nm, SUBLANES, 2, nh)
    x = x.transpose(0, 6, 1, 4, 2, 3, 5)
    return x.reshape(nb, nh, nq * 16, nm * 2)


def _rwkv_post_kernel(y_ref, v_ref, sg_ref, g2_ref, rk_ref, ln_ref, o_ref):
    d = y_ref.shape[1]
    gate = jnp.dot(sg_ref[...], g2_ref[...], preferred_element_type=F32)
    n = RWKV_HEAD_DIM
    y = y_ref[...]
    mean = _tile_lanes(_head_sum(y), d) * (1.0 / n)
    yc = y - mean
    var = _tile_lanes(_head_sum(yc * yc), d) * (1.0 / n)
    yn = (yc * lax.rsqrt(var + GN_EPS)) * ln_ref[0:1, :] + ln_ref[1:2, :]
    bonus = _tile_lanes(rk_ref[...], d) * v_ref[...]
    o_ref[...] = ((yn + bonus) * gate).astype(o_ref.dtype)


def rwkv_post(y, v, sg, g2, rk, ln, tm=256):
    m, d = y.shape
    tm = min(tm, m)
    assert m % tm == 0
    row = pl.BlockSpec((tm, d), lambda i: (i, 0))
    return pl.pallas_call(
        _rwkv_post_kernel,
        grid=(m // tm,),
        in_specs=[row, row, pl.BlockSpec((tm, sg.shape[1]), lambda i: (i, 0)), pl.BlockSpec(g2.shape, lambda i: (0, 0)),
                  pl.BlockSpec((tm, LANES), lambda i: (i, 0)), pl.BlockSpec(ln.shape, lambda i: (0, 0))],
        out_specs=row,
        out_shape=jax.ShapeDtypeStruct((m, d), BF16),
        compiler_params=_params("parallel"),
        name="rwkv_post",
    )(y, v, sg, g2, rk, ln)


def normed_shift_mixes(y, g, shift_prev, mu, nb, t):
    d = y.shape[1]
    if t % LANES == 0:
        return norm_shift_mix(y, g, shift_prev, mu, nb, t)
    h = rmsnorm(y, g, F32)
    h3 = h.reshape(nb, t, d)
    x_prev = jnp.concatenate([shift_prev[:, None, :].astype(F32), h3[:, :-1]], axis=1).reshape(nb * t, d)
    return token_shift_mix(h, x_prev, mu), h3[:, -1, :]


def rwkv_time_mix(streams, wts):
    (xr, xw, xk, xv, xa, xg), (xr_s, xw_s, xk_s, xv_s, xa_s, xg_s) = [s[0] for s in streams]
    proj = lambda x, x_s, w, **kw: matmul(x, w, xs=x_s, **kw)
    r = proj(xr, xr_s, wts["w_r"])
    k = proj(xk, xk_s, wts["w_k"])
    v = proj(xv, xv_s, wts["w_v"])
    tw = proj(xw, xw_s, wts["w1"], act="tanh", out_dtypes=(BF16,))
    ta = proj(xa, xa_s, wts["a1"], out_dtypes=(BF16,))
    sg = proj(xg, xg_s, wts["g1"], act="sigmoid", out_dtypes=(BF16,))
    results = []
    for i, (_, s0, nb, t) in enumerate(streams):
        nh = s0.shape[1]
        decay, k2, kneg, bb, rk = rwkv_prep(r[i], k[i], tw[i], ta[i], wts["w2"], wts["a2"], wts["prm"])
        y, s_t = wkv_scan(_state_to_kernel(s0.astype(F32)), decay, k2, bb, kneg, r[i], v[i], nb, t)
        results.append((rwkv_post(y, v[i], sg[i], wts["g2"], rk, wts["ln"]), _state_from_kernel(s_t, nh)))
    return results


def kernel(x_prompt, x_sample, cache_sb_k, cache_sb_v, page_table, state_rwkv_wkv, state_rwkv_shift, norm_mix, norm_mlp, norm_final, sb_w_qkv, sb_w_o, sb_bias, rw_mu, rw_w_r, rw_w_k, rw_w_v, rw_w_o, rw_w0, rw_w1, rw_w2, rw_a0, rw_a1, rw_a2, rw_g1, rw_g2, rw_k_k, rw_k_a, rw_r_k, rw_ln_w, rw_ln_b, mlp_w1, mlp_w2):
    bp, tp, d = x_prompt.shape
    bs, ts, _ = x_sample.shape
    assert ts == 1, "the paged decode attention handles one new token per sequence"
    depth = norm_mix.shape[0]
    nh_rw = d // RWKV_HEAD_DIM
    n_sb_heads = d // SB_HEAD_DIM

    yp = x_prompt.reshape(bp * tp, d)
    ys = x_sample.reshape(bs * ts, d)
    kp_l, vp_l, ks_l, vs_l = [], [], [], []
    wkvp_l, shp_l, wkvs_l, shs_l = [], [], [], []

    w2_stack = mlp_w2.astype(BF16)
    for layer in range(depth):
        li = layer // 2
        if layer % 2 == 0:
            bias = sb_bias[li]
            hp = rmsnorm(yp, norm_mix[layer], BF16)
            hs = rmsnorm(ys, norm_mix[layer], BF16)
            qp, qs = matmul(hp, sb_w_qkv, layer=li, n_off=0, n=d, xs=hs, out_dtypes=(BF16,))
            (kp, kp16), (ks, _) = matmul(hp, sb_w_qkv, layer=li, n_off=d, n=d, xs=hs, out_dtypes=(F32, BF16))
            (vp, vp16), (vs, _) = matmul(hp, sb_w_qkv, layer=li, n_off=2 * d, n=d, xs=hs, out_dtypes=(F32, BF16))
            op = sb_attention_prompt(qp, kp16, vp16, bias, bp, tp)
            os_ = sb_attention_sample(qs, ks, vs, cache_sb_k, cache_sb_v, li, page_table, bias)
            yp, ys = matmul(op, sb_w_o, layer=li, res=yp, xs=os_.astype(BF16), res_s=ys)
            kp_l.append(kp.reshape(bp, tp, n_sb_heads, SB_HEAD_DIM))
            vp_l.append(vp.reshape(bp, tp, n_sb_heads, SB_HEAD_DIM))
            ks_l.append(ks.reshape(bs, ts, n_sb_heads, SB_HEAD_DIM))
            vs_l.append(vs.reshape(bs, ts, n_sb_heads, SB_HEAD_DIM))
        else:
            hm = lambda w: _head_minor_cols(w.astype(BF16), nh_rw)
            g_pad = -rw_g1.shape[-1] % LANES
            wts = {
                "w_r": hm(rw_w_r[li]), "w_k": hm(rw_w_k[li]), "w_v": hm(rw_w_v[li]),
                "w1": rw_w1[li].astype(BF16), "w2": hm(rw_w2[li]),
                "a1": rw_a1[li].astype(BF16), "a2": hm(rw_a2[li]),
                "g1": jnp.pad(rw_g1[li], ((0, 0), (0, g_pad))).astype(BF16),
                "g2": hm(jnp.pad(rw_g2[li], ((0, g_pad), (0, 0)))),
                "prm": jnp.concatenate([_head_minor_vec(p, nh_rw) for p in
                                        (rw_w0[li], rw_a0[li], rw_k_k[li], rw_k_a[li], rw_r_k[li])]
                                       + [jnp.zeros((3, d), F32)], axis=0),
                "ln": jnp.concatenate([_head_minor_vec(rw_ln_w[li], nh_rw), _head_minor_vec(rw_ln_b[li], nh_rw)], axis=0),
            }
            w_o = rw_w_o[li].astype(BF16).reshape(nh_rw, RWKV_HEAD_DIM, d).swapaxes(0, 1).reshape(d, d)
            zero_shift = jnp.zeros((bp, d), F32)
            zero_wkv = jnp.zeros((bp, nh_rw, RWKV_HEAD_DIM, RWKV_HEAD_DIM), F32)
            mix_p, shp = normed_shift_mixes(yp, norm_mix[layer], zero_shift, rw_mu[li], bp, tp)
            mix_s, shs = normed_shift_mixes(ys, norm_mix[layer], state_rwkv_shift[li], rw_mu[li], bs, ts)
            (op, wkvp), (os_, wkvs) = rwkv_time_mix(
                [(mix_p, zero_wkv, bp, tp), (mix_s, state_rwkv_wkv[li], bs, ts)], wts)
            yp, ys = matmul(op, w_o, res=yp, xs=os_, res_s=ys)
            wkvp_l.append(wkvp)
            shp_l.append(shp)
            wkvs_l.append(wkvs)
            shs_l.append(shs)
        hid_p, hid_s = matmul(rmsnorm(yp, norm_mlp[layer], BF16), mlp_w1, layer=layer, act="relu2",
                              xs=rmsnorm(ys, norm_mlp[layer], BF16), out_dtypes=(BF16,))
        yp, ys = matmul(hid_p, w2_stack, layer=layer, res=yp, xs=hid_s, res_s=ys, tk=4096)

    y_prompt = rmsnorm(yp, norm_final, F32).reshape(bp, tp, d)
    y_sample = rmsnorm(ys, norm_final, F32).reshape(bs, ts, d)
    return (y_prompt, y_sample, jnp.stack(kp_l), jnp.stack(vp_l), jnp.stack(ks_l), jnp.stack(vs_l),
            jnp.stack(wkvp_l), jnp.stack(shp_l), jnp.stack(wkvs_l), jnp.stack(shs_l))
```
